```python
import math
import jax
import jax.numpy as jnp
from jax import lax
import numpy as np

D_MODEL = 1024
BATCH = 16
SEQ = 256
DEPTH = 4
DEC_BATCH = 8
DEC_SEQ = 2048
PAST_LEN = 512

GRID_W = 64
D_A = D_MODEL
FOURIER_GROUPS = 4
FOURIER_GROUP_DIM = D_A // FOURIER_GROUPS
D_B = D_MODEL
CONV_K = 3
FC_IN = 2 * D_A + 4 * D_B
SSD_EXPAND = 2
SSD_D_INNER = SSD_EXPAND * D_MODEL
SSD_HEAD_DIM = 64
SSD_HEADS = SSD_D_INNER // SSD_HEAD_DIM
SSD_GROUPS = 8
SSD_D_STATE = 128
SSD_CONV_DIM = SSD_D_INNER + 2 * SSD_GROUPS * SSD_D_STATE
SSD_IN = SSD_D_INNER + SSD_CONV_DIM + 2 * SSD_HEADS
CHUNK = 128
EPS = 1e-6

kernel_name = "hybrid_fourier_shortconv_ssd_diffusion_step"


def rmsnorm(x, g):
    xf = x.astype(jnp.float32)
    y = xf * lax.rsqrt(jnp.mean(xf * xf, axis=-1, keepdims=True) + EPS)
    return (y * g.astype(jnp.float32)).astype(x.dtype)


def dwconv3(u, w, rows, row_len):
    bsz, L, C = u.shape
    ug = u.reshape(bsz, rows, row_len, C)
    up = jnp.pad(ug, ((0, 0), (0, 0), (1, 1), (0, 0)))
    y = up[:, :, :-2] * w[0] + up[:, :, 1:-1] * w[1] + up[:, :, 2:] * w[2]
    return y.reshape(bsz, L, C)


def fc_mixer(h, w_in, conv_w, w_out, rows, row_len):
    bsz, L, _ = h.shape
    proj = h @ w_in
    u_a, z_a, b_b, c_b, v_b, z_b = jnp.split(proj, 6, axis=-1)
    ua = u_a.astype(jnp.float32).reshape(bsz, L, FOURIER_GROUPS, FOURIER_GROUP_DIM)
    y_a = jnp.fft.fftn(ua, axes=(1, 3), norm="ortho").real.reshape(bsz, L, D_A).astype(h.dtype)
    y_a = y_a * jax.nn.silu(z_a)
    y_b = b_b * dwconv3(c_b * v_b, conv_w, rows, row_len)
    y_b = y_b * jax.nn.silu(z_b)
    return jnp.concatenate([y_a, y_b], axis=-1) @ w_out


def ssd_scan(x, dt, a, b_in, c_in, h0):
    bsz, L, H, P = x.shape
    G, N = b_in.shape[2], b_in.shape[3]
    R = H // G
    nc = L // CHUNK
    xd = (x * dt[..., None]).reshape(bsz, nc, CHUNK, G, R, P)
    la_cum = jnp.cumsum((dt * a).reshape(bsz, nc, CHUNK, G, R), axis=2)
    bc = b_in.reshape(bsz, nc, CHUNK, G, N)
    cc = c_in.reshape(bsz, nc, CHUNK, G, N)
    causal = jnp.tril(jnp.ones((CHUNK, CHUNK), dtype=bool))[None, None, :, :, None, None]
    seg = la_cum[:, :, :, None] - la_cum[:, :, None, :]
    decay = jnp.exp(jnp.where(causal, seg, -jnp.inf))
    cb = jnp.einsum('bclgn,bcsgn->bclsg', cc, bc)
    y_diag = jnp.einsum('bclsgr,bcsgrp->bclgrp', decay * cb[..., None], xd)
    decay_to_end = jnp.exp(la_cum[:, :, -1:] - la_cum)
    st = jnp.einsum('bclgn,bclgr,bclgrp->bcgrpn', bc, decay_to_end, xd)
    chunk_decay = jnp.exp(la_cum[:, :, -1])

    def step(hc, inp):
        st_c, dec_c = inp
        return hc * dec_c[..., None, None] + st_c, hc

    h_final, h_in = lax.scan(step, h0.reshape(bsz, G, R, P, N),
                             (jnp.swapaxes(st, 0, 1), jnp.swapaxes(chunk_decay, 0, 1)))
    h_in = jnp.swapaxes(h_in, 0, 1)
    y_off = jnp.einsum('bclgn,bcgrpn,bclgr->bclgrp', cc, h_in, jnp.exp(la_cum))
    y = (y_diag + y_off).reshape(bsz, L, H, P)
    return y, h_final.reshape(bsz, H, P, N)


def ssd_mixer(h, w_in, conv_w, conv_b, dt_bias, a_log, d_skip, norm_g, w_out, h0_f, h0_b):
    bsz, L, _ = h.shape
    proj = h @ w_in
    z = proj[..., :SSD_D_INNER]
    xbc = proj[..., SSD_D_INNER:SSD_D_INNER + SSD_CONV_DIM]
    dt_raw = proj[..., SSD_D_INNER + SSD_CONV_DIM:]
    xbc = jax.nn.silu(dwconv3(xbc, conv_w, 1, L) + conv_b).astype(jnp.float32)
    xs = xbc[..., :SSD_D_INNER].reshape(bsz, L, SSD_HEADS, SSD_HEAD_DIM)
    bs = xbc[..., SSD_D_INNER:SSD_D_INNER + SSD_GROUPS * SSD_D_STATE].reshape(bsz, L, SSD_GROUPS, SSD_D_STATE)
    cs = xbc[..., SSD_D_INNER + SSD_GROUPS * SSD_D_STATE:].reshape(bsz, L, SSD_GROUPS, SSD_D_STATE)
    dt = jax.nn.softplus(dt_raw.astype(jnp.float32).reshape(bsz, L, 2, SSD_HEADS)
                         + dt_bias.astype(jnp.float32))
    a = -jnp.exp(a_log.astype(jnp.float32))
    y_f, hf = ssd_scan(xs, dt[:, :, 0], a[0], bs, cs, h0_f.astype(jnp.float32))
    flip = lambda t: jnp.flip(t, axis=1)
    y_b, hb = ssd_scan(flip(xs), flip(dt[:, :, 1]), a[1], flip(bs), flip(cs), h0_b.astype(jnp.float32))
    y = y_f + flip(y_b) + d_skip.astype(jnp.float32)[:, None] * xs
    y = y.reshape(bsz, L, SSD_D_INNER) * jax.nn.silu(z.astype(jnp.float32))
    yg = y.reshape(bsz, L, SSD_GROUPS, SSD_D_INNER // SSD_GROUPS)
    yg = yg * lax.rsqrt(jnp.mean(yg * yg, axis=-1, keepdims=True) + EPS)
    y = yg.reshape(bsz, L, SSD_D_INNER) * norm_g.astype(jnp.float32)
    return y.astype(h.dtype) @ w_out, hf.astype(h.dtype), hb.astype(h.dtype)


def trunk(x, cond, rows, row_len, h0, w_mod, b_mod, norm_g, fc_w_in, fc_conv_w, fc_w_out,
          ssd_w_in, ssd_conv_w, ssd_conv_b, ssd_dt_bias, ssd_a_log, ssd_d, ssd_norm_g,
          ssd_w_out, final_norm_g):
    cond_act = jax.nn.silu(cond)
    states = []
    for layer in range(DEPTH):
        shift, scale, gate = jnp.split(cond_act @ w_mod[layer] + b_mod[layer], 3, axis=-1)
        hm = rmsnorm(x, norm_g[layer]) * (1 + scale[:, None]) + shift[:, None]
        i = layer // 2
        if layer % 2 == 0:
            out = fc_mixer(hm, fc_w_in[i], fc_conv_w[i], fc_w_out[i], rows, row_len)
        else:
            out, hf, hb = ssd_mixer(hm, ssd_w_in[i], ssd_conv_w[i], ssd_conv_b[i], ssd_dt_bias[i],
                                    ssd_a_log[i], ssd_d[i], ssd_norm_g[i], ssd_w_out[i],
                                    h0[:, i, 0], h0[:, i, 1])
            states.append(jnp.stack([hf, hb], axis=1))
        x = x + gate[:, None] * out
    return rmsnorm(x, final_norm_g), jnp.stack(states, axis=1)


def setup_inputs(seed: int = 0) -> dict:
    key = jax.random.key(seed)
    ks = jax.random.split(key, 24)
    nrm = lambda k, shape, s: jax.random.normal(k, shape, jnp.float32) * s
    ne = (DEPTH + 1) // 2
    no = DEPTH // 2
    dt0 = jnp.exp(jax.random.uniform(ks[14], (no, 2, SSD_HEADS), jnp.float32,
                                     minval=math.log(1e-3), maxval=math.log(1e-1)))
    return {
        "x_prompt": nrm(ks[0], (BATCH, SEQ, D_MODEL), 1.0),
        "x_sample": nrm(ks[1], (DEC_BATCH, DEC_SEQ, D_MODEL), 1.0),
        "state_ssm": nrm(ks[2], (DEC_BATCH, no, 2, SSD_HEADS, SSD_HEAD_DIM, SSD_D_STATE), 0.3),
        "c": nrm(ks[3], (DEC_BATCH, D_MODEL), 1.0),
        "c_ctx": nrm(ks[4], (D_MODEL,), 1.0),
        "w_mod": nrm(ks[5], (DEPTH, D_MODEL, 3 * D_MODEL), D_MODEL ** -0.5),
        "b_mod": nrm(ks[6], (DEPTH, 3 * D_MODEL), 0.02),
        "norm_g": 1.0 + nrm(ks[7], (DEPTH, D_MODEL), 0.02),
        "fc_w_in": nrm(ks[8], (ne, D_MODEL, FC_IN), D_MODEL ** -0.5),
        "fc_conv_w": nrm(ks[9], (ne, CONV_K, D_B), CONV_K ** -0.5),
        "fc_w_out": nrm(ks[10], (ne, D_A + D_B, D_MODEL), (D_A + D_B) ** -0.5),
        "ssd_w_in": nrm(ks[11], (no, D_MODEL, SSD_IN), D_MODEL ** -0.5),
        "ssd_conv_w": nrm(ks[12], (no, CONV_K, SSD_CONV_DIM), CONV_K ** -0.5),
        "ssd_conv_b": nrm(ks[13], (no, SSD_CONV_DIM), 0.02),
        "ssd_dt_bias": dt0 + jnp.log(-jnp.expm1(-dt0)),
        "ssd_a_log": jnp.log(jax.random.uniform(ks[15], (no, 2, SSD_HEADS), jnp.float32, minval=1.0, maxval=16.0)),
        "ssd_d": 1.0 + nrm(ks[16], (no, SSD_HEADS), 0.1),
        "ssd_norm_g": 1.0 + nrm(ks[17], (no, SSD_D_INNER), 0.02),
        "ssd_w_out": nrm(ks[18], (no, SSD_D_INNER, D_MODEL), SSD_D_INNER ** -0.5),
        "final_norm_g": 1.0 + nrm(ks[19], (D_MODEL,), 0.02),
    }


def reference(x_prompt, x_sample, state_ssm, c, c_ctx, w_mod, b_mod, norm_g, fc_w_in, fc_conv_w,
              fc_w_out, ssd_w_in, ssd_conv_w, ssd_conv_b, ssd_dt_bias, ssd_a_log, ssd_d,
              ssd_norm_g, ssd_w_out, final_norm_g):
    params = (w_mod, b_mod, norm_g, fc_w_in, fc_conv_w, fc_w_out, ssd_w_in, ssd_conv_w,
              ssd_conv_b, ssd_dt_bias, ssd_a_log, ssd_d, ssd_norm_g, ssd_w_out, final_norm_g)
    h0_ctx = jnp.zeros((x_prompt.shape[0],) + state_ssm.shape[1:], x_prompt.dtype)
    y_prompt, new_state_ssm = trunk(x_prompt, c_ctx[None, :], 1, x_prompt.shape[1], h0_ctx, *params)
    rows = x_sample.shape[1] // GRID_W
    y_sample, _ = trunk(x_sample, c, rows, GRID_W, state_ssm, *params)
    return (y_prompt, y_sample, new_state_ssm)
```

```python
import functools
import math

import jax
import jax.numpy as jnp
from jax import lax
from jax.experimental import pallas as pl
from jax.experimental.pallas import tpu as pltpu

F32 = jnp.float32
BF16 = jnp.bfloat16

EPS = 1e-6
GRID_W = 64
FOURIER_GROUPS = 4
CHUNK = 128
HEADS_PER_GROUP = 4
LANES = 128
BF16_SUBLANES = 16
COND_ROWS = 16
VMEM_LIMIT = 56 * 1024 * 1024


def _params(n_grid):
    return pltpu.CompilerParams(
        dimension_semantics=("arbitrary",) * n_grid, vmem_limit_bytes=VMEM_LIMIT)


def _dot(a, b):
    return jnp.dot(a, b, preferred_element_type=F32)


def _dot_nt(a, b):
    return lax.dot_general(a, b, (((1,), (1,)), ((), ())), preferred_element_type=F32)


def _sigmoid(x):
    return 1.0 / (1.0 + jnp.exp(-x))


def _silu(x):
    return x * _sigmoid(x)


def _rms(x):
    return x * lax.rsqrt(jnp.mean(x * x, axis=-1, keepdims=True) + EPS)


def _modnorm(x, g_ref, mod_ref):
    d = x.shape[-1]
    shift = mod_ref[:, 0:d]
    scale = mod_ref[:, d:2 * d]
    return _rms(x) * g_ref[...] * (1.0 + scale) + shift


def _residual(x, out, mod_ref, fg_ref):
    d = x.shape[-1]
    xn = x + mod_ref[:, 2 * d:3 * d] * out
    if fg_ref is not None:
        xn = _rms(xn) * fg_ref[...]
    return xn


def _mod_kernel(c_ref, w_ref, b_ref, o_ref):
    act = _silu(c_ref[...]).astype(BF16)
    o_ref[...] = _dot(act, w_ref[...].astype(BF16)) + b_ref[...]


def _modulation(cond, w_mod, b_mod):
    depth, d, d3 = w_mod.shape
    nb = d3 // d
    return pl.pallas_call(
        _mod_kernel,
        grid=(depth, nb),
        in_specs=[
            pl.BlockSpec((COND_ROWS, d), lambda l, j: (0, 0)),
            pl.BlockSpec((None, d, d), lambda l, j: (l, 0, j)),
            pl.BlockSpec((None, 1, d), lambda l, j: (l, 0, j)),
        ],
        out_specs=pl.BlockSpec((None, COND_ROWS, d), lambda l, j: (l, 0, j)),
        out_shape=jax.ShapeDtypeStruct((depth, COND_ROWS, d3), F32),
        compiler_params=_params(2),
        name="modulation",
    )(cond, w_mod, b_mod.reshape(depth, 1, d3))


def _fc_in_kernel(x_ref, g_ref, mod_ref, w_ref, cw_ref, dft_ref,
                  xc_ref, xs_ref, ga_ref, yb_ref, *, row_len, col_block):
    tm, d = x_ref.shape
    gd = d // FOURIER_GROUPS
    hm = _modnorm(x_ref[...], g_ref, mod_ref).astype(BF16)
    for g in range(FOURIER_GROUPS):
        u = _dot(hm, w_ref[:, g * gd:(g + 1) * gd]).astype(BF16)
        t = _dot(u, dft_ref[...])
        xc_ref[:, g * gd:(g + 1) * gd] = t[:, :gd].astype(BF16)
        xs_ref[:, g * gd:(g + 1) * gd] = t[:, gd:].astype(BF16)
    pos = lax.rem(lax.broadcasted_iota(jnp.int32, (tm, 1), 0), row_len)
    first = pos == 0
    last = pos == row_len - 1
    for j in range(d // col_block):
        lo = j * col_block
        za = _dot(hm, w_ref[:, d + lo:d + lo + col_block])
        ga_ref[:, lo:lo + col_block] = _silu(za).astype(BF16)
        bb = _dot(hm, w_ref[:, 2 * d + lo:2 * d + lo + col_block])
        cc = _dot(hm, w_ref[:, 3 * d + lo:3 * d + lo + col_block])
        vv = _dot(hm, w_ref[:, 4 * d + lo:4 * d + lo + col_block])
        zb = _dot(hm, w_ref[:, 5 * d + lo:5 * d + lo + col_block])
        u = cc * vv
        up = jnp.where(first, 0.0, pltpu.roll(u, 1, 0))
        un = jnp.where(last, 0.0, pltpu.roll(u, tm - 1, 0))
        cw = cw_ref[:, lo:lo + col_block]
        y = up * cw[0:1] + u * cw[1:2] + un * cw[2:3]
        yb_ref[:, lo:lo + col_block] = (bb * y * _silu(zb)).astype(BF16)


def _fc_out_kernel(dc_ref, ds_ref, xc_ref, xs_ref, ga_ref, yb_ref, w_ref, x_ref, mod_ref,
                   *rest, final):
    fg_ref = rest[0] if final else None
    o_ref = rest[-1]
    d = x_ref.shape[1]
    ya = _dot(dc_ref[...], xc_ref[...]) + _dot(ds_ref[...], xs_ref[...])
    h1 = (ya * ga_ref[...].astype(F32)).astype(BF16)
    out = _dot(h1, w_ref[0:d, :]) + _dot(yb_ref[...], w_ref[d:2 * d, :])
    o_ref[...] = _residual(x_ref[...], out, mod_ref, fg_ref)


def _dft_tables(n, dtype=BF16):
    k = jnp.arange(n, dtype=jnp.int32)
    m = (k[:, None] * k[None, :]) % n
    ang = m.astype(F32) * (2.0 * math.pi / n)
    s = 1.0 / math.sqrt(n)
    return (jnp.cos(ang) * s).astype(dtype), (jnp.sin(ang) * s).astype(dtype)


def _fc_layer(x2d, seq, mod3, mod_row, norm_g, w_in, conv_w, w_out, row_len, final_g):
    t, d = x2d.shape
    tm = min(512, t)
    tiles_per_seq = max(seq.length // tm, 1)
    gd = d // FOURIER_GROUPS
    cc, sc = _dft_tables(gd)
    dft_ch = jnp.concatenate([cc, sc], axis=1)
    row_of_tile = lambda i: mod_row(i // tiles_per_seq)
    const = lambda *_: (0, 0)
    tok = pl.BlockSpec((tm, d), lambda i: (i, 0))
    xc, xs, ga, yb = pl.pallas_call(
        functools.partial(_fc_in_kernel, row_len=row_len, col_block=min(512, d)),
        grid=(t // tm,),
        in_specs=[
            tok,
            pl.BlockSpec((1, d), const),
            pl.BlockSpec((None, 1, 3 * d), lambda i: (row_of_tile(i), 0, 0)),
            pl.BlockSpec((d, 6 * d), const, pipeline_mode=pl.Buffered(1)),
            pl.BlockSpec((3, d), const),
            pl.BlockSpec((gd, 2 * gd), const),
        ],
        out_specs=[tok, tok, tok, tok],
        out_shape=[jax.ShapeDtypeStruct((t, d), BF16)] * 4,
        compiler_params=_params(1),
        name="fc_in",
    )(x2d, norm_g[None], mod3, w_in, conv_w, dft_ch)

    ln = seq.length
    tr = min(512, ln)
    rt = ln // tr
    cl, sl = _dft_tables(ln)
    final = final_g is not None
    row = pl.BlockSpec((tr, d), lambda s, r: (s * rt + r, 0))
    whole = pl.BlockSpec((ln, d), lambda s, r: (s, 0))
    in_specs = [
        pl.BlockSpec((tr, ln), lambda s, r: (r, 0)),
        pl.BlockSpec((tr, ln), lambda s, r: (r, 0)),
        whole, whole, row, row,
        pl.BlockSpec((2 * d, d), lambda s, r: (0, 0), pipeline_mode=pl.Buffered(1)),
        row,
        pl.BlockSpec((None, 1, 3 * d), lambda s, r: (mod_row(s), 0, 0)),
    ]
    args = [cl, -sl, xc, xs, ga, yb, w_out, x2d, mod3]
    if final:
        in_specs.append(pl.BlockSpec((1, d), lambda s, r: (0, 0)))
        args.append(final_g[None])
    return pl.pallas_call(
        functools.partial(_fc_out_kernel, final=final),
        grid=(seq.count, rt),
        in_specs=in_specs,
        out_specs=row,
        out_shape=jax.ShapeDtypeStruct((t, d), F32),
        compiler_params=_params(2),
        name="fc_out",
    )(*args)


def _ssd_in_kernel(x_ref, g_ref, mod_ref, w_ref, wdt_ref, dtb_ref, proj_ref, dt_ref, *, col_block):
    tm = x_ref.shape[0]
    hm = _modnorm(x_ref[...], g_ref, mod_ref).astype(BF16)
    for j in range(w_ref.shape[1] // col_block):
        sl = slice(j * col_block, (j + 1) * col_block)
        proj_ref[:, sl] = _dot(hm, w_ref[:, sl]).astype(BF16)
    v = _dot_nt(wdt_ref[...], hm) + dtb_ref[...]
    dt = jnp.maximum(v, 0.0) + jnp.log(1.0 + jnp.exp(-jnp.abs(v)))
    for k in range(tm // CHUNK):
        dt_ref[k] = dt[:, k * CHUNK:(k + 1) * CHUNK]


def _ssd_scan_kernel(z_ref, x_ref, b_ref, c_ref, dt_ref, alog_ref,
                     cwx_ref, cwb_ref, cwc_ref, cbx_ref, cbb_ref, cbc_ref, d_ref, g_ref,
                     *rest, has_h0, emit_state):
    rest = list(rest)
    h0_ref = rest.pop(0) if has_h0 else None
    y_ref = rest.pop(0)
    st_ref = rest.pop(0) if emit_state else None
    xst_s, bs_s, cs_s, cum_s, yp_s, sf_s, sb_s, dcol_s = rest

    ln, gw = x_ref.shape
    n = b_ref.shape[1]
    nc = ln // CHUNK
    hp = gw // HEADS_PER_GROUP
    r2 = 2 * HEADS_PER_GROUP

    if has_h0:
        sf_s[...] = h0_ref[0].reshape(gw, n)
        sb_s[...] = h0_ref[1].reshape(gw, n)
    else:
        sf_s[...] = jnp.zeros((gw, n), F32)
        sb_s[...] = jnp.zeros((gw, n), F32)
    dcol_s[...] = jnp.broadcast_to(d_ref[...], (gw, LANES))
    a8 = -jnp.exp(alog_ref[...])

    rid = lax.broadcasted_iota(jnp.int32, (CHUNK, 1), 0)
    lane8 = lax.broadcasted_iota(jnp.int32, (r2, LANES), 1)
    row8 = lax.broadcasted_iota(jnp.int32, (r2, LANES), 0)
    li = lax.broadcasted_iota(jnp.int32, (CHUNK, CHUNK), 0)
    si = lax.broadcasted_iota(jnp.int32, (CHUNK, CHUNK), 1)

    def conv_silu(ref, w_ref, bias_ref, c, s):
        u = ref[pl.ds(s, CHUNK), :].astype(F32)
        sp = pl.multiple_of(jnp.maximum(s - BF16_SUBLANES, 0), BF16_SUBLANES)
        sn = pl.multiple_of(jnp.minimum(s + CHUNK, ln - BF16_SUBLANES), BF16_SUBLANES)
        prev = ref[pl.ds(sp, BF16_SUBLANES), :].astype(F32)[BF16_SUBLANES - 1:BF16_SUBLANES, :]
        nxt = ref[pl.ds(sn, BF16_SUBLANES), :].astype(F32)[0:1, :]
        prev = jnp.where(c > 0, prev, 0.0)
        nxt = jnp.where(c < nc - 1, nxt, 0.0)
        up = jnp.where(rid == 0, prev, pltpu.roll(u, 1, 0))
        un = jnp.where(rid == CHUNK - 1, nxt, pltpu.roll(u, CHUNK - 1, 0))
        w = w_ref[...]
        return _silu(up * w[0:1] + u * w[1:2] + un * w[2:3] + bias_ref[...])

    def expand(rows):
        return jnp.concatenate(
            [jnp.broadcast_to(rows[h:h + 1, :], (hp, LANES)) for h in range(HEADS_PER_GROUP)],
            axis=0)

    def forward_chunk(c, carry):
        s = pl.multiple_of(c * CHUNK, CHUNK)
        xs = conv_silu(x_ref, cwx_ref, cbx_ref, c, s)
        bm = conv_silu(b_ref, cwb_ref, cbb_ref, c, s).astype(BF16)
        cm = conv_silu(c_ref, cwc_ref, cbc_ref, c, s).astype(BF16)
        xst = xs.T
        xst_b = xst.astype(BF16)
        xst_s[c] = xst_b
        bs_s[pl.ds(s, CHUNK), :] = bm
        cs_s[pl.ds(s, CHUNK), :] = cm

        dt8 = dt_ref[c]
        pre = dt8 * a8
        suf = pre
        k = 1
        while k < CHUNK:
            pre = pre + jnp.where(lane8 >= k, pltpu.roll(pre, k, 1), 0.0)
            suf = suf + jnp.where(lane8 < CHUNK - k, pltpu.roll(suf, CHUNK - k, 1), 0.0)
            k *= 2
        cum8 = jnp.where(row8 < HEADS_PER_GROUP, pre, suf)
        cum_s[c] = cum8
        cum_t = cum8.T

        cb = _dot_nt(cm, bm)
        parts = []
        for h in range(HEADS_PER_GROUP):
            hb = HEADS_PER_GROUP + h
            arg = jnp.where(si <= li,
                            cum_t[:, h:h + 1] - cum8[h:h + 1, :],
                            cum_t[:, hb:hb + 1] - cum8[hb:hb + 1, :])
            dtf = dt8[h:h + 1, :]
            dtb = dt8[hb:hb + 1, :]
            dts = jnp.where(si < li, dtf, jnp.where(si > li, dtb, dtf + dtb))
            w = (cb * jnp.exp(arg) * dts).astype(BF16)
            parts.append(_dot_nt(xst_b[h * hp:(h + 1) * hp, :], w))
        y_t = jnp.concatenate(parts, axis=0)

        tot = jnp.broadcast_to(cum8[:, CHUNK - 1:CHUNK], (r2, LANES))[0:HEADS_PER_GROUP]
        cum4 = cum8[0:HEADS_PER_GROUP]
        coef = dt8[0:HEADS_PER_GROUP] * jnp.exp(tot - cum4)
        st = _dot((xst * expand(coef)).astype(BF16), bm)
        s_in = sf_s[...]
        y_t = y_t + _dot_nt(s_in.astype(BF16), cm) * expand(jnp.exp(cum4))
        sf_s[...] = s_in * expand(jnp.exp(tot)) + st
        yp_s[c] = y_t
        return carry

    lax.fori_loop(0, nc, forward_chunk, 0)

    def backward_chunk(i, carry):
        c = nc - 1 - i
        s = pl.multiple_of(c * CHUNK, CHUNK)
        xst = xst_s[c].astype(F32)
        bm = bs_s[pl.ds(s, CHUNK), :]
        cm = cs_s[pl.ds(s, CHUNK), :]
        dt4 = dt_ref[c][HEADS_PER_GROUP:r2]
        cum4 = cum_s[c][HEADS_PER_GROUP:r2]
        tot = jnp.broadcast_to(cum4[:, 0:1], (HEADS_PER_GROUP, LANES))
        coef = dt4 * jnp.exp(tot - cum4)
        st = _dot((xst * expand(coef)).astype(BF16), bm)
        s_in = sb_s[...]
        y_t = (yp_s[c] + _dot_nt(s_in.astype(BF16), cm) * expand(jnp.exp(cum4))
               + dcol_s[...] * xst)
        sb_s[...] = s_in * expand(jnp.exp(tot)) + st
        y = y_t.T * _silu(z_ref[pl.ds(s, CHUNK), :].astype(F32))
        y_ref[pl.ds(s, CHUNK), :] = (_rms(y) * g_ref[...]).astype(BF16)
        return carry

    lax.fori_loop(0, nc, backward_chunk, 0)

    if emit_state:
        st_ref[0] = sf_s[...].reshape(HEADS_PER_GROUP, hp, n)
        st_ref[1] = sb_s[...].reshape(HEADS_PER_GROUP, hp, n)


def _ssd_out_kernel(y_ref, w_ref, x_ref, mod_ref, *rest, final):
    fg_ref = rest[0] if final else None
    o_ref = rest[-1]
    out = _dot(y_ref[...], w_ref[...])
    o_ref[...] = _residual(x_ref[...], out, mod_ref, fg_ref)


def _ssd_layer(x2d, seq, mod3, mod_row, norm_g, w_zxbc, wdt_t, dt_bias_col, alog_rows,
               conv_w, conv_b, d_col, gn_g, w_out, h0, layer_idx, emit_state, final_g):
    t, d = x2d.shape
    d_inner = w_out.shape[0]
    heads2 = wdt_t.shape[0]
    groups = heads2 // (2 * HEADS_PER_GROUP)
    gw = d_inner // groups
    n = (w_zxbc.shape[1] - 2 * d_inner) // (2 * groups)
    hp = gw // HEADS_PER_GROUP
    ln = seq.length
    nc = ln // CHUNK
    tm = min(512, t)
    tiles_per_seq = max(ln // tm, 1)
    row_of_tile = lambda i: mod_row(i // tiles_per_seq)
    const = lambda *_: (0, 0)
    tok = pl.BlockSpec((tm, d), lambda i: (i, 0))
    wcols = w_zxbc.shape[1]

    proj, dt = pl.pallas_call(
        functools.partial(_ssd_in_kernel, col_block=min(1024, wcols)),
        grid=(t // tm,),
        in_specs=[
            tok,
            pl.BlockSpec((1, d), const),
            pl.BlockSpec((None, 1, 3 * d), lambda i: (row_of_tile(i), 0, 0)),
            pl.BlockSpec((d, wcols), const, pipeline_mode=pl.Buffered(1)),
            pl.BlockSpec((heads2, d), const),
            pl.BlockSpec((heads2, 1), const),
        ],
        out_specs=[
            pl.BlockSpec((tm, wcols), lambda i: (i, 0)),
            pl.BlockSpec((tm // CHUNK, heads2, CHUNK), lambda i: (i, 0, 0)),
        ],
        out_shape=[
            jax.ShapeDtypeStruct((t, wcols), BF16),
            jax.ShapeDtypeStruct((t // CHUNK, heads2, CHUNK), F32),
        ],
        compiler_params=_params(1),
        name="ssd_in",
    )(x2d, norm_g[None], mod3, w_zxbc, wdt_t, dt_bias_col)

    r2 = 2 * HEADS_PER_GROUP
    xg0 = d_inner // gw
    bg0 = 2 * d_inner // n
    cg0 = bg0 + groups
    wb0 = d_inner // n
    wc0 = wb0 + groups
    has_h0 = h0 is not None
    in_specs = [
        pl.BlockSpec((ln, gw), lambda s, g: (s, g)),
        pl.BlockSpec((ln, gw), lambda s, g: (s, xg0 + g)),
        pl.BlockSpec((ln, n), lambda s, g: (s, bg0 + g)),
        pl.BlockSpec((ln, n), lambda s, g: (s, cg0 + g)),
        pl.BlockSpec((nc, r2, CHUNK), lambda s, g: (s, g, 0)),
        pl.BlockSpec((None, r2, LANES), lambda s, g: (g, 0, 0)),
        pl.BlockSpec((3, gw), lambda s, g: (0, g)),
        pl.BlockSpec((3, n), lambda s, g: (0, wb0 + g)),
        pl.BlockSpec((3, n), lambda s, g: (0, wc0 + g)),
        pl.BlockSpec((1, gw), lambda s, g: (0, g)),
        pl.BlockSpec((1, n), lambda s, g: (0, wb0 + g)),
        pl.BlockSpec((1, n), lambda s, g: (0, wc0 + g)),
        pl.BlockSpec((gw, 1), lambda s, g: (g, 0)),
        pl.BlockSpec((1, gw), lambda s, g: (0, g)),
    ]
    args = [proj, proj, proj, proj, dt, alog_rows, conv_w, conv_w, conv_w,
            conv_b[None], conv_b[None], conv_b[None], d_col, gn_g[None]]
    if has_h0:
        in_specs.append(pl.BlockSpec((None, None, 2, HEADS_PER_GROUP, hp, n),
                                     lambda s, g: (s, layer_idx, 0, g, 0, 0)))
        args.append(h0)
    out_specs = [pl.BlockSpec((ln, gw), lambda s, g: (s, g))]
    out_shape = [jax.ShapeDtypeStruct((t, d_inner), BF16)]
    if emit_state:
        out_specs.append(pl.BlockSpec((None, 2, HEADS_PER_GROUP, hp, n),
                                      lambda s, g: (s, 0, g, 0, 0)))
        out_shape.append(jax.ShapeDtypeStruct(
            (seq.count, 2, groups * HEADS_PER_GROUP, hp, n), F32))
    res = pl.pallas_call(
        functools.partial(_ssd_scan_kernel, has_h0=has_h0, emit_state=emit_state),
        grid=(seq.count, groups),
        in_specs=in_specs,
        out_specs=out_specs,
        out_shape=out_shape,
        scratch_shapes=[
            pltpu.VMEM((nc, gw, CHUNK), BF16),
            pltpu.VMEM((ln, n), BF16),
            pltpu.VMEM((ln, n), BF16),
            pltpu.VMEM((nc, r2, CHUNK), F32),
            pltpu.VMEM((nc, gw, CHUNK), F32),
            pltpu.VMEM((gw, n), F32),
            pltpu.VMEM((gw, n), F32),
            pltpu.VMEM((gw, LANES), F32),
        ],
        compiler_params=_params(2),
        name="ssd_scan",
    )(*args)
    y, state = (res[0], res[1]) if emit_state else (res[0], None)

    final = final_g is not None
    in_specs = [
        pl.BlockSpec((tm, d_inner), lambda i: (i, 0)),
        pl.BlockSpec((d_inner, d), const, pipeline_mode=pl.Buffered(1)),
        tok,
        pl.BlockSpec((None, 1, 3 * d), lambda i: (row_of_tile(i), 0, 0)),
    ]
    args = [y, w_out, x2d, mod3]
    if final:
        in_specs.append(pl.BlockSpec((1, d), const))
        args.append(final_g[None])
    x_new = pl.pallas_call(
        functools.partial(_ssd_out_kernel, final=final),
        grid=(t // tm,),
        in_specs=in_specs,
        out_specs=tok,
        out_shape=jax.ShapeDtypeStruct((t, d), F32),
        compiler_params=_params(1),
        name="ssd_out",
    )(*args)
    return x_new, state


class _Seqs:
    def __init__(self, count, length):
        self.count = count
        self.length = length


def _trunk(x, mod, mod_row, row_len, h0, emit_state, weights):
    (norm_g, fc_w_in, fc_conv_w, fc_w_out, ssd_w, ssd_wdt_t, ssd_dtb, ssd_alog, ssd_conv_w,
     ssd_conv_b, ssd_dcol, ssd_norm_g, ssd_w_out, final_norm_g) = weights
    bsz, ln, d = x.shape
    seq = _Seqs(bsz, ln)
    x2d = x.reshape(bsz * ln, d)
    depth = norm_g.shape[0]
    states = []
    for layer in range(depth):
        i = layer // 2
        mod3 = mod[layer].reshape(COND_ROWS, 1, 3 * d)
        final_g = final_norm_g if layer == depth - 1 else None
        if layer % 2 == 0:
            x2d = _fc_layer(x2d, seq, mod3, mod_row, norm_g[layer], fc_w_in[i], fc_conv_w[i],
                            fc_w_out[i], row_len, final_g)
        else:
            x2d, st = _ssd_layer(x2d, seq, mod3, mod_row, norm_g[layer], ssd_w[i], ssd_wdt_t[i],
                                 ssd_dtb[i], ssd_alog[i], ssd_conv_w[i], ssd_conv_b[i],
                                 ssd_dcol[i], ssd_norm_g[i], ssd_w_out[i], h0, i, emit_state,
                                 final_g)
            states.append(st)
    return x2d.reshape(bsz, ln, d), states


def kernel(x_prompt, x_sample, state_ssm, c, c_ctx, w_mod, b_mod, norm_g, fc_w_in, fc_conv_w,
           fc_w_out, ssd_w_in, ssd_conv_w, ssd_conv_b, ssd_dt_bias, ssd_a_log, ssd_d,
           ssd_norm_g, ssd_w_out, final_norm_g):
    d = x_prompt.shape[-1]
    dec_batch = x_sample.shape[0]
    no, _, heads = ssd_a_log.shape
    d_inner = ssd_w_out.shape[1]
    conv_dim = ssd_conv_w.shape[-1]
    hp = d_inner // heads
    groups = heads // HEADS_PER_GROUP
    assert dec_batch + 1 <= COND_ROWS and heads % HEADS_PER_GROUP == 0

    cond = jnp.zeros((COND_ROWS, d), F32).at[:dec_batch].set(c).at[dec_batch].set(c_ctx)
    mod = _modulation(cond, w_mod, b_mod)

    zx = d_inner + conv_dim
    ssd_w = ssd_w_in[:, :, :zx].astype(BF16)
    perm = jnp.arange(2 * heads).reshape(2, groups, HEADS_PER_GROUP).transpose(1, 0, 2).reshape(-1)
    ssd_wdt_t = jnp.swapaxes(ssd_w_in[:, :, zx:][:, :, perm], 1, 2).astype(BF16)
    ssd_dtb = ssd_dt_bias.reshape(no, 2 * heads)[:, perm][:, :, None]
    alog = ssd_a_log.reshape(no, 2 * heads)[:, perm].reshape(no, groups, 2 * HEADS_PER_GROUP, 1)
    ssd_alog = jnp.broadcast_to(alog, (no, groups, 2 * HEADS_PER_GROUP, LANES))
    ssd_dcol = jnp.repeat(ssd_d, hp, axis=1)[:, :, None]
    weights = (norm_g, fc_w_in.astype(BF16), fc_conv_w, fc_w_out.astype(BF16), ssd_w, ssd_wdt_t,
               ssd_dtb, ssd_alog, ssd_conv_w, ssd_conv_b, ssd_dcol, ssd_norm_g,
               ssd_w_out.astype(BF16), final_norm_g)

    y_prompt, states = _trunk(x_prompt, mod, lambda s: dec_batch, x_prompt.shape[1], None, True,
                              weights)
    y_sample, _ = _trunk(x_sample, mod, lambda s: s, GRID_W, state_ssm, False, weights)
    return y_prompt, y_sample, jnp.stack(states, axis=1)
```

```python
import functools
import math

import jax
import jax.numpy as jnp
from jax import lax
from jax.experimental import pallas as pl
from jax.experimental.pallas import tpu as pltpu

F32 = jnp.float32
BF16 = jnp.bfloat16

EPS = 1e-6
GRID_W = 64
FOURIER_GROUPS = 4
CHUNK = 128
HEADS_PER_GROUP = 4
LANES = 128
F32_SUBLANES = 8
COND_ROWS = 16
VMEM_LIMIT = 56 * 1024 * 1024


def _params(n_grid):
    return pltpu.CompilerParams(
        dimension_semantics=("arbitrary",) * n_grid, vmem_limit_bytes=VMEM_LIMIT)


def _dot(a, b):
    return jnp.dot(a, b, preferred_element_type=F32)


def _dot_nt(a, b):
    return lax.dot_general(a, b, (((1,), (1,)), ((), ())), preferred_element_type=F32)


def _sigmoid(x):
    return 1.0 / (1.0 + jnp.exp(-x))


def _silu(x):
    return x * _sigmoid(x)


def _rms(x):
    return x * lax.rsqrt(jnp.mean(x * x, axis=-1, keepdims=True) + EPS)


def _modnorm(x, g_ref, mod_ref):
    d = x.shape[-1]
    shift = mod_ref[:, 0:d]
    scale = mod_ref[:, d:2 * d]
    return _rms(x) * g_ref[...] * (1.0 + scale) + shift


def _residual(x, out, mod_ref, fg_ref):
    d = x.shape[-1]
    xn = x + mod_ref[:, 2 * d:3 * d] * out
    if fg_ref is not None:
        xn = _rms(xn) * fg_ref[...]
    return xn


def _mod_kernel(c_ref, w_ref, b_ref, o_ref):
    act = _silu(c_ref[...]).astype(BF16)
    o_ref[...] = _dot(act, w_ref[...].astype(BF16)) + b_ref[...]


def _modulation(cond, w_mod, b_mod):
    depth, d, d3 = w_mod.shape
    nb = d3 // d
    return pl.pallas_call(
        _mod_kernel,
        grid=(depth, nb),
        in_specs=[
            pl.BlockSpec((COND_ROWS, d), lambda l, j: (0, 0)),
            pl.BlockSpec((None, d, d), lambda l, j: (l, 0, j)),
            pl.BlockSpec((None, 1, d), lambda l, j: (l, 0, j)),
        ],
        out_specs=pl.BlockSpec((None, COND_ROWS, d), lambda l, j: (l, 0, j)),
        out_shape=jax.ShapeDtypeStruct((depth, COND_ROWS, d3), F32),
        compiler_params=_params(2),
        name="modulation",
    )(cond, w_mod, b_mod.reshape(depth, 1, d3))


def _fc_in_kernel(x_ref, g_ref, mod_ref, w_ref, cw_ref, dft_ref,
                  xc_ref, xs_ref, ga_ref, yb_ref, *, row_len, col_block):
    tm, d = x_ref.shape
    gd = d // FOURIER_GROUPS
    hm = _modnorm(x_ref[...], g_ref, mod_ref).astype(BF16)
    for g in range(FOURIER_GROUPS):
        u = _dot(hm, w_ref[:, g * gd:(g + 1) * gd]).astype(BF16)
        t = _dot(u, dft_ref[...])
        xc_ref[:, g * gd:(g + 1) * gd] = t[:, :gd].astype(BF16)
        xs_ref[:, g * gd:(g + 1) * gd] = t[:, gd:].astype(BF16)
    pos = lax.rem(lax.broadcasted_iota(jnp.int32, (tm, 1), 0), row_len)
    first = pos == 0
    last = pos == row_len - 1
    for j in range(d // col_block):
        lo = j * col_block
        za = _dot(hm, w_ref[:, d + lo:d + lo + col_block])
        ga_ref[:, lo:lo + col_block] = _silu(za).astype(BF16)
        bb = _dot(hm, w_ref[:, 2 * d + lo:2 * d + lo + col_block])
        cc = _dot(hm, w_ref[:, 3 * d + lo:3 * d + lo + col_block])
        vv = _dot(hm, w_ref[:, 4 * d + lo:4 * d + lo + col_block])
        zb = _dot(hm, w_ref[:, 5 * d + lo:5 * d + lo + col_block])
        u = cc * vv
        up = jnp.where(first, 0.0, pltpu.roll(u, 1, 0))
        un = jnp.where(last, 0.0, pltpu.roll(u, tm - 1, 0))
        cw = cw_ref[:, lo:lo + col_block]
        y = up * cw[0:1] + u * cw[1:2] + un * cw[2:3]
        yb_ref[:, lo:lo + col_block] = (bb * y * _silu(zb)).astype(BF16)


def _fc_out_kernel(dc_ref, ds_ref, xc_ref, xs_ref, ga_ref, yb_ref, w_ref, x_ref, mod_ref,
                   *rest, final):
    fg_ref = rest[0] if final else None
    o_ref = rest[-1]
    d = x_ref.shape[1]
    ya = _dot(dc_ref[...], xc_ref[...]) + _dot(ds_ref[...], xs_ref[...])
    h1 = (ya * ga_ref[...].astype(F32)).astype(BF16)
    out = _dot(h1, w_ref[0:d, :]) + _dot(yb_ref[...], w_ref[d:2 * d, :])
    o_ref[...] = _residual(x_ref[...], out, mod_ref, fg_ref)


def _dft_tables(n, dtype=BF16):
    k = jnp.arange(n, dtype=jnp.int32)
    m = (k[:, None] * k[None, :]) % n
    ang = m.astype(F32) * (2.0 * math.pi / n)
    s = 1.0 / math.sqrt(n)
    return (jnp.cos(ang) * s).astype(dtype), (jnp.sin(ang) * s).astype(dtype)


def _fc_layer(x2d, seq, mod3, mod_row, norm_g, w_in, conv_w, w_out, row_len, final_g):
    t, d = x2d.shape
    tm = min(512, t)
    tiles_per_seq = max(seq.length // tm, 1)
    gd = d // FOURIER_GROUPS
    cc, sc = _dft_tables(gd)
    dft_ch = jnp.concatenate([cc, sc], axis=1)
    row_of_tile = lambda i: mod_row(i // tiles_per_seq)
    const = lambda *_: (0, 0)
    tok = pl.BlockSpec((tm, d), lambda i: (i, 0))
    xc, xs, ga, yb = pl.pallas_call(
        functools.partial(_fc_in_kernel, row_len=row_len, col_block=min(512, d)),
        grid=(t // tm,),
        in_specs=[
            tok,
            pl.BlockSpec((1, d), const),
            pl.BlockSpec((None, 1, 3 * d), lambda i: (row_of_tile(i), 0, 0)),
            pl.BlockSpec((d, 6 * d), const, pipeline_mode=pl.Buffered(1)),
            pl.BlockSpec((3, d), const),
            pl.BlockSpec((gd, 2 * gd), const),
        ],
        out_specs=[tok, tok, tok, tok],
        out_shape=[jax.ShapeDtypeStruct((t, d), BF16)] * 4,
        compiler_params=_params(1),
        name="fc_in",
    )(x2d, norm_g[None], mod3, w_in, conv_w, dft_ch)

    ln = seq.length
    tr = min(512, ln)
    rt = ln // tr
    cl, sl = _dft_tables(ln)
    final = final_g is not None
    row = pl.BlockSpec((tr, d), lambda s, r: (s * rt + r, 0))
    whole = pl.BlockSpec((ln, d), lambda s, r: (s, 0))
    in_specs = [
        pl.BlockSpec((tr, ln), lambda s, r: (r, 0)),
        pl.BlockSpec((tr, ln), lambda s, r: (r, 0)),
        whole, whole, row, row,
        pl.BlockSpec((2 * d, d), lambda s, r: (0, 0), pipeline_mode=pl.Buffered(1)),
        row,
        pl.BlockSpec((None, 1, 3 * d), lambda s, r: (mod_row(s), 0, 0)),
    ]
    args = [cl, -sl, xc, xs, ga, yb, w_out, x2d, mod3]
    if final:
        in_specs.append(pl.BlockSpec((1, d), lambda s, r: (0, 0)))
        args.append(final_g[None])
    return pl.pallas_call(
        functools.partial(_fc_out_kernel, final=final),
        grid=(seq.count, rt),
        in_specs=in_specs,
        out_specs=row,
        out_shape=jax.ShapeDtypeStruct((t, d), F32),
        compiler_params=_params(2),
        name="fc_out",
    )(*args)


TAB_DT, TAB_CUM, TAB_COEF, TAB_E, TAB_CD = range(5)


def _ssd_in_kernel(x_ref, xp_ref, xn_ref, g_ref, mod_ref, w_ref, wdt_ref, dtb_ref, alog_ref,
                   cw_ref, cb_ref, proj_ref, tab_ref, *, seq_len, d_inner, col_block):
    tm = x_ref.shape[0]
    halo = xp_ref.shape[0]
    hm_f = _modnorm(x_ref[...], g_ref, mod_ref)
    hm = hm_f.astype(BF16)
    for j in range(d_inner // col_block):
        sl = slice(j * col_block, (j + 1) * col_block)
        proj_ref[:, sl] = _silu(_dot(hm, w_ref[:, sl])).astype(BF16)

    hm_ext = jnp.concatenate(
        [_modnorm(xp_ref[...], g_ref, mod_ref), hm_f, _modnorm(xn_ref[...], g_ref, mod_ref)],
        axis=0).astype(BF16)
    pos = lax.rem(pl.program_id(0) * tm + lax.broadcasted_iota(jnp.int32, (tm, 1), 0), seq_len)
    first = pos == 0
    last = pos == seq_len - 1
    ext = tm + 2 * halo
    for j in range((w_ref.shape[1] - d_inner) // col_block):
        lo = j * col_block
        p = _dot(hm_ext, w_ref[:, d_inner + lo:d_inner + lo + col_block])
        up = jnp.where(first, 0.0, pltpu.roll(p, 1, 0)[halo:halo + tm])
        un = jnp.where(last, 0.0, pltpu.roll(p, ext - 1, 0)[halo:halo + tm])
        cw = cw_ref[:, lo:lo + col_block]
        y = up * cw[0:1] + p[halo:halo + tm] * cw[1:2] + un * cw[2:3] + cb_ref[:, lo:lo + col_block]
        proj_ref[:, d_inner + lo:d_inner + lo + col_block] = _silu(y).astype(BF16)

    v = _dot_nt(wdt_ref[...], hm) + dtb_ref[...]
    dt_all = jnp.maximum(v, 0.0) + jnp.log(1.0 + jnp.exp(-jnp.abs(v)))
    heads2 = dt_all.shape[0]
    a = -jnp.exp(alog_ref[...])
    lane = lax.broadcasted_iota(jnp.int32, (heads2, CHUNK), 1)
    row = lax.broadcasted_iota(jnp.int32, (heads2, CHUNK), 0)
    is_fwd = (row & (2 * HEADS_PER_GROUP - 1)) < HEADS_PER_GROUP
    for k in range(tm // CHUNK):
        dt = dt_all[:, k * CHUNK:(k + 1) * CHUNK]
        pre = dt * a
        suf = pre
        sh = 1
        while sh < CHUNK:
            pre = pre + jnp.where(lane >= sh, pltpu.roll(pre, sh, 1), 0.0)
            suf = suf + jnp.where(lane < CHUNK - sh, pltpu.roll(suf, CHUNK - sh, 1), 0.0)
            sh *= 2
        cum = jnp.where(is_fwd, pre, suf)
        tot = jnp.where(is_fwd, jnp.broadcast_to(cum[:, CHUNK - 1:CHUNK], cum.shape),
                        jnp.broadcast_to(cum[:, 0:1], cum.shape))
        tab_ref[k, TAB_DT] = dt
        tab_ref[k, TAB_CUM] = cum
        tab_ref[k, TAB_COEF] = dt * jnp.exp(tot - cum)
        tab_ref[k, TAB_E] = jnp.exp(cum)
        tab_ref[k, TAB_CD] = jnp.exp(tot)


def _ssd_scan_kernel(sz_ref, x_ref, b_ref, c_ref, tab_ref, d_ref, g_ref,
                     *rest, has_h0, emit_state, unroll):
    rest = list(rest)
    h0_ref = rest.pop(0) if has_h0 else None
    y_ref = rest.pop(0)
    st_ref = rest.pop(0) if emit_state else None
    yp_s, stf_s, stb_s, hf_s, hb_s, dcol_s = rest

    ln, gw = x_ref.shape
    n = b_ref.shape[1]
    nc = ln // CHUNK
    hp = gw // HEADS_PER_GROUP
    nh = HEADS_PER_GROUP
    r2 = 2 * nh

    dcol_s[...] = jnp.broadcast_to(d_ref[...], (gw, LANES))

    li = lax.broadcasted_iota(jnp.int32, (CHUNK, CHUNK), 0)
    si = lax.broadcasted_iota(jnp.int32, (CHUNK, CHUNK), 1)

    def expand(rows4):
        return jnp.concatenate(
            [jnp.broadcast_to(rows4[h:h + 1, :], (hp, LANES)) for h in range(nh)], axis=0)

    def local_chunk(c, carry):
        s = pl.multiple_of(c * CHUNK, CHUNK)
        xst = x_ref[pl.ds(s, CHUNK), :].astype(F32).T
        xst_b = xst.astype(BF16)
        bm = b_ref[pl.ds(s, CHUNK), :]
        cm = c_ref[pl.ds(s, CHUNK), :]
        dt8 = tab_ref[c, TAB_DT]
        cum8 = tab_ref[c, TAB_CUM]
        cum_t = cum8.T

        cb = _dot_nt(cm, bm)
        parts = []
        for h in range(HEADS_PER_GROUP):
            hb = HEADS_PER_GROUP + h
            arg = jnp.where(si <= li,
                            cum_t[:, h:h + 1] - cum8[h:h + 1, :],
                            cum_t[:, hb:hb + 1] - cum8[hb:hb + 1, :])
            dtf = dt8[h:h + 1, :]
            dtb = dt8[hb:hb + 1, :]
            dts = jnp.where(si < li, dtf, jnp.where(si > li, dtb, dtf + dtb))
            w = (cb * jnp.exp(arg) * dts).astype(BF16)
            parts.append(_dot_nt(xst_b[h * hp:(h + 1) * hp, :], w))
        yp_s[c] = jnp.concatenate(parts, axis=0) + dcol_s[...] * xst

        coef8 = tab_ref[c, TAB_COEF]
        xdw = jnp.concatenate([xst * expand(coef8[0:nh]), xst * expand(coef8[nh:r2])], axis=0)
        st = _dot(xdw.astype(BF16), bm)
        stf_s[c] = st[0:gw]
        stb_s[c] = st[gw:2 * gw]
        return carry

    lax.fori_loop(0, nc, local_chunk, 0, unroll=unroll)

    if has_h0:
        h0f = h0_ref[0].reshape(gw, n)
        h0b = h0_ref[1].reshape(gw, n)
    else:
        h0f = jnp.zeros((gw, n), F32)
        h0b = h0f

    def fwd_state(c, h):
        hf_s[c] = h.astype(BF16)
        return h * expand(tab_ref[c, TAB_CD][0:nh]) + stf_s[c]

    def bwd_state(i, h):
        c = nc - 1 - i
        hb_s[c] = h.astype(BF16)
        return h * expand(tab_ref[c, TAB_CD][nh:r2]) + stb_s[c]

    hf = lax.fori_loop(0, nc, fwd_state, h0f)
    hb = lax.fori_loop(0, nc, bwd_state, h0b)
    if emit_state:
        st_ref[0] = hf.reshape(nh, hp, n)
        st_ref[1] = hb.reshape(nh, hp, n)

    def output_chunk(c, carry):
        s = pl.multiple_of(c * CHUNK, CHUNK)
        hin = jnp.concatenate([hf_s[c], hb_s[c]], axis=0)
        yo = _dot_nt(hin, c_ref[pl.ds(s, CHUNK), :])
        e8 = tab_ref[c, TAB_E]
        y_t = yp_s[c] + yo[0:gw] * expand(e8[0:nh]) + yo[gw:2 * gw] * expand(e8[nh:r2])
        y = y_t.T * sz_ref[pl.ds(s, CHUNK), :].astype(F32)
        y_ref[pl.ds(s, CHUNK), :] = (_rms(y) * g_ref[...]).astype(BF16)
        return carry

    lax.fori_loop(0, nc, output_chunk, 0, unroll=unroll)


def _ssd_out_kernel(y_ref, w_ref, x_ref, mod_ref, *rest, final):
    fg_ref = rest[0] if final else None
    o_ref = rest[-1]
    out = _dot(y_ref[...], w_ref[...])
    o_ref[...] = _residual(x_ref[...], out, mod_ref, fg_ref)


def _ssd_layer(x2d, seq, mod3, mod_row, norm_g, w_zxbc, wdt_t, dt_bias_col, alog_col,
               conv_w, conv_b, d_col, gn_g, w_out, h0, layer_idx, emit_state, final_g):
    t, d = x2d.shape
    d_inner = w_out.shape[0]
    heads2 = wdt_t.shape[0]
    groups = heads2 // (2 * HEADS_PER_GROUP)
    gw = d_inner // groups
    n = (w_zxbc.shape[1] - 2 * d_inner) // (2 * groups)
    hp = gw // HEADS_PER_GROUP
    ln = seq.length
    nc = ln // CHUNK
    tm = min(512, t)
    tiles_per_seq = max(ln // tm, 1)
    row_of_tile = lambda i: mod_row(i // tiles_per_seq)
    const = lambda *_: (0, 0)
    tok = pl.BlockSpec((tm, d), lambda i: (i, 0))
    wcols = w_zxbc.shape[1]

    halo_blocks = tm // F32_SUBLANES
    last_halo = t // F32_SUBLANES - 1
    n_tab = TAB_CD + 1
    proj, tab = pl.pallas_call(
        functools.partial(_ssd_in_kernel, seq_len=ln, d_inner=d_inner, col_block=min(512, d_inner)),
        grid=(t // tm,),
        in_specs=[
            tok,
            pl.BlockSpec((F32_SUBLANES, d), lambda i: (jnp.maximum(i * halo_blocks - 1, 0), 0)),
            pl.BlockSpec((F32_SUBLANES, d),
                         lambda i: (jnp.minimum((i + 1) * halo_blocks, last_halo), 0)),
            pl.BlockSpec((1, d), const),
            pl.BlockSpec((None, 1, 3 * d), lambda i: (row_of_tile(i), 0, 0)),
            pl.BlockSpec((d, wcols), const, pipeline_mode=pl.Buffered(1)),
            pl.BlockSpec((heads2, d), const),
            pl.BlockSpec((heads2, 1), const),
            pl.BlockSpec((heads2, 1), const),
            pl.BlockSpec((3, wcols - d_inner), const),
            pl.BlockSpec((1, wcols - d_inner), const),
        ],
        out_specs=[
            pl.BlockSpec((tm, wcols), lambda i: (i, 0)),
            pl.BlockSpec((tm // CHUNK, n_tab, heads2, CHUNK), lambda i: (i, 0, 0, 0)),
        ],
        out_shape=[
            jax.ShapeDtypeStruct((t, wcols), BF16),
            jax.ShapeDtypeStruct((t // CHUNK, n_tab, heads2, CHUNK), F32),
        ],
        compiler_params=_params(1),
        name="ssd_in",
    )(x2d, x2d, x2d, norm_g[None], mod3, w_zxbc, wdt_t, dt_bias_col, alog_col, conv_w,
      conv_b[None])

    r2 = 2 * HEADS_PER_GROUP
    xg0 = d_inner // gw
    bg0 = 2 * d_inner // n
    cg0 = bg0 + groups
    has_h0 = h0 is not None
    in_specs = [
        pl.BlockSpec((ln, gw), lambda s, g: (s, g)),
        pl.BlockSpec((ln, gw), lambda s, g: (s, xg0 + g)),
        pl.BlockSpec((ln, n), lambda s, g: (s, bg0 + g)),
        pl.BlockSpec((ln, n), lambda s, g: (s, cg0 + g)),
        pl.BlockSpec((nc, n_tab, r2, CHUNK), lambda s, g: (s, 0, g, 0)),
        pl.BlockSpec((gw, 1), lambda s, g: (g, 0)),
        pl.BlockSpec((1, gw), lambda s, g: (0, g)),
    ]
    args = [proj, proj, proj, proj, tab, d_col, gn_g[None]]
    if has_h0:
        in_specs.append(pl.BlockSpec((None, None, 2, HEADS_PER_GROUP, hp, n),
                                     lambda s, g: (s, layer_idx, 0, g, 0, 0)))
        args.append(h0)
    out_specs = [pl.BlockSpec((ln, gw), lambda s, g: (s, g))]
    out_shape = [jax.ShapeDtypeStruct((t, d_inner), BF16)]
    if emit_state:
        out_specs.append(pl.BlockSpec((None, 2, HEADS_PER_GROUP, hp, n),
                                      lambda s, g: (s, 0, g, 0, 0)))
        out_shape.append(jax.ShapeDtypeStruct(
            (seq.count, 2, groups * HEADS_PER_GROUP, hp, n), F32))
    res = pl.pallas_call(
        functools.partial(_ssd_scan_kernel, has_h0=has_h0, emit_state=emit_state,
                          unroll=min(4, nc)),
        grid=(seq.count, groups),
        in_specs=in_specs,
        out_specs=out_specs,
        out_shape=out_shape,
        scratch_shapes=[
            pltpu.VMEM((nc, gw, CHUNK), F32),
            pltpu.VMEM((nc, gw, n), F32),
            pltpu.VMEM((nc, gw, n), F32),
            pltpu.VMEM((nc, gw, n), BF16),
            pltpu.VMEM((nc, gw, n), BF16),
            pltpu.VMEM((gw, LANES), F32),
        ],
        compiler_params=_params(2),
        name="ssd_scan",
    )(*args)
    y, state = (res[0], res[1]) if emit_state else (res[0], None)

    final = final_g is not None
    in_specs = [
        pl.BlockSpec((tm, d_inner), lambda i: (i, 0)),
        pl.BlockSpec((d_inner, d), const, pipeline_mode=pl.Buffered(1)),
        tok,
        pl.BlockSpec((None, 1, 3 * d), lambda i: (row_of_tile(i), 0, 0)),
    ]
    args = [y, w_out, x2d, mod3]
    if final:
        in_specs.append(pl.BlockSpec((1, d), const))
        args.append(final_g[None])
    x_new = pl.pallas_call(
        functools.partial(_ssd_out_kernel, final=final),
        grid=(t // tm,),
        in_specs=in_specs,
        out_specs=tok,
        out_shape=jax.ShapeDtypeStruct((t, d), F32),
        compiler_params=_params(1),
        name="ssd_out",
    )(*args)
    return x_new, state


class _Seqs:
    def __init__(self, count, length):
        self.count = count
        self.length = length


def _trunk(x, mod, mod_row, row_len, h0, emit_state, weights):
    (norm_g, fc_w_in, fc_conv_w, fc_w_out, ssd_w, ssd_wdt_t, ssd_dtb, ssd_alog, ssd_conv_w,
     ssd_conv_b, ssd_dcol, ssd_norm_g, ssd_w_out, final_norm_g) = weights
    bsz, ln, d = x.shape
    seq = _Seqs(bsz, ln)
    x2d = x.reshape(bsz * ln, d)
    depth = norm_g.shape[0]
    states = []
    for layer in range(depth):
        i = layer // 2
        mod3 = mod[layer].reshape(COND_ROWS, 1, 3 * d)
        final_g = final_norm_g if layer == depth - 1 else None
        if layer % 2 == 0:
            x2d = _fc_layer(x2d, seq, mod3, mod_row, norm_g[layer], fc_w_in[i], fc_conv_w[i],
                            fc_w_out[i], row_len, final_g)
        else:
            x2d, st = _ssd_layer(x2d, seq, mod3, mod_row, norm_g[layer], ssd_w[i], ssd_wdt_t[i],
                                 ssd_dtb[i], ssd_alog[i], ssd_conv_w[i], ssd_conv_b[i],
                                 ssd_dcol[i], ssd_norm_g[i], ssd_w_out[i], h0, i, emit_state,
                                 final_g)
            states.append(st)
    return x2d.reshape(bsz, ln, d), states


def kernel(x_prompt, x_sample, state_ssm, c, c_ctx, w_mod, b_mod, norm_g, fc_w_in, fc_conv_w,
           fc_w_out, ssd_w_in, ssd_conv_w, ssd_conv_b, ssd_dt_bias, ssd_a_log, ssd_d,
           ssd_norm_g, ssd_w_out, final_norm_g):
    d = x_prompt.shape[-1]
    dec_batch = x_sample.shape[0]
    no, _, heads = ssd_a_log.shape
    d_inner = ssd_w_out.shape[1]
    conv_dim = ssd_conv_w.shape[-1]
    hp = d_inner // heads
    groups = heads // HEADS_PER_GROUP
    assert dec_batch + 1 <= COND_ROWS and heads % HEADS_PER_GROUP == 0

    cond = jnp.zeros((COND_ROWS, d), F32).at[:dec_batch].set(c).at[dec_batch].set(c_ctx)
    mod = _modulation(cond, w_mod, b_mod)

    zx = d_inner + conv_dim
    ssd_w = ssd_w_in[:, :, :zx].astype(BF16)
    perm = jnp.arange(2 * heads).reshape(2, groups, HEADS_PER_GROUP).transpose(1, 0, 2).reshape(-1)
    ssd_wdt_t = jnp.swapaxes(ssd_w_in[:, :, zx:][:, :, perm], 1, 2).astype(BF16)
    ssd_dtb = ssd_dt_bias.reshape(no, 2 * heads)[:, perm][:, :, None]
    ssd_alog = ssd_a_log.reshape(no, 2 * heads)[:, perm][:, :, None]
    ssd_dcol = jnp.repeat(ssd_d, hp, axis=1)[:, :, None]
    weights = (norm_g, fc_w_in.astype(BF16), fc_conv_w, fc_w_out.astype(BF16), ssd_w, ssd_wdt_t,
               ssd_dtb, ssd_alog, ssd_conv_w, ssd_conv_b, ssd_dcol, ssd_norm_g,
               ssd_w_out.astype(BF16), final_norm_g)

    y_prompt, states = _trunk(x_prompt, mod, lambda s: dec_batch, x_prompt.shape[1], None, True,
                              weights)
    y_sample, _ = _trunk(x_sample, mod, lambda s: s, GRID_W, state_ssm, False, weights)
    return y_prompt, y_sample, jnp.stack(states, axis=1)
```

```python
import functools
import math

import jax
import jax.numpy as jnp
from jax import lax
from jax.experimental import pallas as pl
from jax.experimental.pallas import tpu as pltpu

F32 = jnp.float32
BF16 = jnp.bfloat16

EPS = 1e-6
GRID_W = 64
FOURIER_GROUPS = 4
DFT_INNER = 64
CHUNK = 128
HEADS_PER_GROUP = 4
LANES = 128
SSD_IN_TOKENS = 2048
SSD_IN_ROWS = 1024
SSD_IN_SUB_ROWS = 256
SCAN_CELL_CHUNKS = 8
COND_ROWS = 16
VMEM_LIMIT = 56 * 1024 * 1024


def _params(n_grid):
    return pltpu.CompilerParams(
        dimension_semantics=("arbitrary",) * n_grid, vmem_limit_bytes=VMEM_LIMIT)


def _dot(a, b):
    return jnp.dot(a, b, preferred_element_type=F32)


def _dot_nt(a, b):
    return lax.dot_general(a, b, (((1,), (1,)), ((), ())), preferred_element_type=F32)


def _sigmoid(x):
    return 1.0 / (1.0 + jnp.exp(-x))


def _silu(x):
    return x * _sigmoid(x)


def _rms(x):
    return x * lax.rsqrt(jnp.mean(x * x, axis=-1, keepdims=True) + EPS)


def _modnorm(x, g_ref, mod_ref):
    d = x.shape[-1]
    shift = mod_ref[:, 0:d]
    scale = mod_ref[:, d:2 * d]
    return _rms(x) * g_ref[...] * (1.0 + scale) + shift


def _residual(x, out, mod_ref, fg_ref):
    d = x.shape[-1]
    xn = x + mod_ref[:, 2 * d:3 * d] * out
    if fg_ref is not None:
        xn = _rms(xn) * fg_ref[...]
    return xn


def _mod_kernel(c_ref, w_ref, b_ref, o_ref):
    act = _silu(c_ref[...]).astype(BF16)
    o_ref[...] = _dot(act, w_ref[...].astype(BF16)) + b_ref[...]


def _modulation(cond, w_mod, b_mod):
    depth, d, d3 = w_mod.shape
    nb = d3 // d
    return pl.pallas_call(
        _mod_kernel,
        grid=(depth, nb),
        in_specs=[
            pl.BlockSpec((COND_ROWS, d), lambda l, j: (0, 0)),
            pl.BlockSpec((None, d, d), lambda l, j: (l, 0, j)),
            pl.BlockSpec((None, 1, d), lambda l, j: (l, 0, j)),
        ],
        out_specs=pl.BlockSpec((None, COND_ROWS, d), lambda l, j: (l, 0, j)),
        out_shape=jax.ShapeDtypeStruct((depth, COND_ROWS, d3), F32),
        compiler_params=_params(2),
        name="modulation",
    )(cond, w_mod, b_mod.reshape(depth, 1, d3))


def _fc_in_kernel(x_ref, g_ref, mod_ref, w_ref, cw_ref, dft_ref,
                  xc_ref, xs_ref, ga_ref, yb_ref, *, row_len, col_block):
    tm, d = x_ref.shape
    gd = d // FOURIER_GROUPS
    hm = _modnorm(x_ref[...], g_ref, mod_ref).astype(BF16)
    ua = _dot(hm, w_ref[:, 0:d]).astype(BF16)
    for g in range(FOURIER_GROUPS):
        t = _dot(ua[:, g * gd:(g + 1) * gd], dft_ref[...])
        xc_ref[:, g * gd:(g + 1) * gd] = t[:, :gd].astype(BF16)
        xs_ref[:, g * gd:(g + 1) * gd] = t[:, gd:].astype(BF16)
    pos = lax.rem(lax.broadcasted_iota(jnp.int32, (tm, 1), 0), row_len)
    first = pos == 0
    last = pos == row_len - 1
    for j in range(d // col_block):
        lo = j * col_block
        za = _dot(hm, w_ref[:, d + lo:d + lo + col_block])
        ga_ref[:, lo:lo + col_block] = _silu(za).astype(BF16)
        bb = _dot(hm, w_ref[:, 2 * d + lo:2 * d + lo + col_block])
        cc = _dot(hm, w_ref[:, 3 * d + lo:3 * d + lo + col_block])
        vv = _dot(hm, w_ref[:, 4 * d + lo:4 * d + lo + col_block])
        zb = _dot(hm, w_ref[:, 5 * d + lo:5 * d + lo + col_block])
        u = cc * vv
        up = jnp.where(first, 0.0, pltpu.roll(u, 1, 0))
        un = jnp.where(last, 0.0, pltpu.roll(u, tm - 1, 0))
        cw = cw_ref[:, lo:lo + col_block]
        y = up * cw[0:1] + u * cw[1:2] + un * cw[2:3]
        yb_ref[:, lo:lo + col_block] = (bb * y * _silu(zb)).astype(BF16)


def _fc_out_kernel(dc_ref, ds_ref, xc_ref, xs_ref, ga_ref, yb_ref, w_ref, x_ref, mod_ref,
                   *rest, final):
    fg_ref = rest[0] if final else None
    o_ref = rest[-1]
    d = x_ref.shape[1]
    ya = _dot(dc_ref[...], xc_ref[...]) + _dot(ds_ref[...], xs_ref[...])
    h1 = (ya * ga_ref[...].astype(F32)).astype(BF16)
    out = _dot(h1, w_ref[0:d, :]) + _dot(yb_ref[...], w_ref[d:2 * d, :])
    o_ref[...] = _residual(x_ref[...], out, mod_ref, fg_ref)


def _dft_tables(n, dtype=BF16):
    nb = DFT_INNER if n % DFT_INNER == 0 else 1
    k = jnp.arange(n, dtype=jnp.int32)[:, None]

    def base(cols):
        ang = ((k * cols[None, :]) % n).astype(F32) * (2.0 * math.pi / n)
        return jnp.cos(ang), jnp.sin(ang)

    ca, sa = base(jnp.arange(n // nb, dtype=jnp.int32) * nb)
    cb, sb = base(jnp.arange(nb, dtype=jnp.int32))
    ca, sa, cb, sb = ca[:, :, None], sa[:, :, None], cb[:, None, :], sb[:, None, :]
    s = 1.0 / math.sqrt(n)
    cos = ((ca * cb - sa * sb) * s).reshape(n, n)
    sin = ((sa * cb + ca * sb) * s).reshape(n, n)
    return cos.astype(dtype), sin.astype(dtype)


def _fc_layer(x2d, seq, mod3, mod_row, norm_g, w_in, conv_w, w_out, li, row_len, final_g):
    t, d = x2d.shape
    tm = min(512, t)
    tiles_per_seq = max(seq.length // tm, 1)
    gd = d // FOURIER_GROUPS
    cc, sc = _dft_tables(gd)
    dft_ch = jnp.concatenate([cc, sc], axis=1)
    row_of_tile = lambda i: mod_row(i // tiles_per_seq)
    const = lambda *_: (0, 0)
    tok = pl.BlockSpec((tm, d), lambda i: (i, 0))
    xc, xs, ga, yb = pl.pallas_call(
        functools.partial(_fc_in_kernel, row_len=row_len, col_block=min(512, d)),
        grid=(t // tm,),
        in_specs=[
            tok,
            pl.BlockSpec((1, d), const),
            pl.BlockSpec((None, 1, 3 * d), lambda i: (row_of_tile(i), 0, 0)),
            pl.BlockSpec((None, d, 6 * d), lambda i: (li, 0, 0), pipeline_mode=pl.Buffered(1)),
            pl.BlockSpec((3, d), const),
            pl.BlockSpec((gd, 2 * gd), const),
        ],
        out_specs=[tok, tok, tok, tok],
        out_shape=[jax.ShapeDtypeStruct((t, d), BF16)] * 4,
        compiler_params=_params(1),
        name="fc_in",
    )(x2d, norm_g[None], mod3, w_in, conv_w, dft_ch)

    ln = seq.length
    tr = min(512, ln)
    rt = ln // tr
    cl, sl = _dft_tables(ln)
    final = final_g is not None
    row = pl.BlockSpec((tr, d), lambda s, r: (s * rt + r, 0))
    whole = pl.BlockSpec((ln, d), lambda s, r: (s, 0))
    in_specs = [
        pl.BlockSpec((tr, ln), lambda s, r: (r, 0)),
        pl.BlockSpec((tr, ln), lambda s, r: (r, 0)),
        whole, whole, row, row,
        pl.BlockSpec((None, 2 * d, d), lambda s, r: (li, 0, 0), pipeline_mode=pl.Buffered(1)),
        row,
        pl.BlockSpec((None, 1, 3 * d), lambda s, r: (mod_row(s), 0, 0)),
    ]
    args = [cl, -sl, xc, xs, ga, yb, w_out, x2d, mod3]
    if final:
        in_specs.append(pl.BlockSpec((1, d), lambda s, r: (0, 0)))
        args.append(final_g[None])
    return pl.pallas_call(
        functools.partial(_fc_out_kernel, final=final),
        grid=(seq.count, rt),
        in_specs=in_specs,
        out_specs=row,
        out_shape=jax.ShapeDtypeStruct((t, d), F32),
        compiler_params=_params(2),
        name="fc_out",
    )(*args)


TAB_DT, TAB_CUM2, TAB_ROW2, TAB_COEF, TAB_E, TAB_CD = range(6)
LOG2_E = math.log2(math.e)
LOG2_FLOOR = -1e30


def _ssd_in_kernel(x_ref, g_ref, mod_ref, wt_ref, wdt_ref, dtb_ref, alog_ref, cw_ref,
                   proj_ref, tab_ref, hm_s, *, seq_len, gate_blocks, sub_rows):
    j = pl.program_id(1)
    nt = x_ref.shape[0]
    rows = wt_ref.shape[0]
    nc = nt // CHUNK

    def emit(r0, y):
        yb = y.astype(BF16)
        for k in range(nc):
            proj_ref[k, pl.ds(r0, sub_rows), :] = yb[:, k * CHUNK:(k + 1) * CHUNK]

    @pl.when(j == 0)
    def _():
        hm = _modnorm(x_ref[...], g_ref, mod_ref).astype(BF16)
        hm_s[...] = hm
        _scan_tables(hm, wdt_ref, dtb_ref, alog_ref, tab_ref)

    @pl.when(j < gate_blocks)
    def _():
        for r0 in range(0, rows, sub_rows):
            emit(r0, _silu(_dot_nt(wt_ref[r0:r0 + sub_rows, :], hm_s[...])))

    @pl.when(j >= gate_blocks)
    def _():
        lane0 = lax.broadcasted_iota(jnp.int32, (sub_rows, LANES), 1)

        @pl.loop(0, rows // sub_rows)
        def _(i):
            r0 = pl.multiple_of(i * sub_rows, sub_rows)
            p = _dot_nt(wt_ref[pl.ds(r0, sub_rows), :], hm_s[...])
            left = pltpu.roll(p, 1, 1)
            right = pltpu.roll(p, nt - 1, 1)
            lcols = [left[:, q:q + LANES] for q in range(0, nt, LANES)]
            rcols = [right[:, q:q + LANES] for q in range(0, nt, LANES)]
            for q in range(0, nt, seq_len):
                lcols[q // LANES] = jnp.where(lane0 == 0, 0.0, lcols[q // LANES])
                e = (q + seq_len) // LANES - 1
                rcols[e] = jnp.where(lane0 == LANES - 1, 0.0, rcols[e])
            left = jnp.concatenate(lcols, axis=1)
            right = jnp.concatenate(rcols, axis=1)
            cw = cw_ref[pl.ds(r0, sub_rows), :]
            y = left * cw[:, 0:1] + p * cw[:, 1:2] + right * cw[:, 2:3] + cw[:, 3:4]
            emit(r0, _silu(y))


def _scan_tables(hm, wdt_ref, dtb_ref, alog_ref, tab_ref):
    v = _dot_nt(wdt_ref[...], hm) + dtb_ref[...]
    dt_all = jnp.maximum(v, 0.0) + jnp.log(1.0 + jnp.exp(-jnp.abs(v)))
    heads2, nt = dt_all.shape
    nc = nt // CHUNK
    la_all = dt_all * -jnp.exp(alog_ref[...])
    dt = jnp.concatenate([dt_all[:, k * CHUNK:(k + 1) * CHUNK] for k in range(nc)], axis=0)
    la = jnp.concatenate([la_all[:, k * CHUNK:(k + 1) * CHUNK] for k in range(nc)], axis=0)
    p0 = la.astype(BF16)
    r1 = la - p0.astype(F32)
    p1 = r1.astype(BF16)
    p2 = (r1 - p1.astype(F32)).astype(BF16)
    ji = lax.broadcasted_iota(jnp.int32, (CHUNK, CHUNK), 0)
    li = lax.broadcasted_iota(jnp.int32, (CHUNK, CHUNK), 1)
    upto = jnp.where(ji <= li, 1.0, 0.0).astype(BF16)
    from_ = jnp.where(ji >= li, 1.0, 0.0).astype(BF16)
    pre = _dot(p0, upto) + _dot(p1, upto) + _dot(p2, upto)
    suf = _dot(p0, from_) + _dot(p1, from_) + _dot(p2, from_)
    row = lax.broadcasted_iota(jnp.int32, la.shape, 0)
    is_fwd = (row & (2 * HEADS_PER_GROUP - 1)) < HEADS_PER_GROUP
    cum = jnp.where(is_fwd, pre, suf)
    tot = jnp.where(is_fwd, jnp.broadcast_to(cum[:, CHUNK - 1:CHUNK], cum.shape),
                    jnp.broadcast_to(cum[:, 0:1], cum.shape))
    coef = dt * jnp.exp(tot - cum)
    e = jnp.exp(cum)
    cd = jnp.exp(tot)
    cum2 = cum * LOG2_E
    row2 = cum2 - jnp.maximum(jnp.log2(dt), LOG2_FLOOR)
    for k in range(nc):
        rows = slice(k * heads2, (k + 1) * heads2)
        tab_ref[k, TAB_DT] = dt[rows]
        tab_ref[k, TAB_CUM2] = cum2[rows]
        tab_ref[k, TAB_ROW2] = row2[rows]
        tab_ref[k, TAB_COEF] = coef[rows]
        tab_ref[k, TAB_E] = e[rows]
        tab_ref[k, TAB_CD] = cd[rows]


def _ssd_scan_kernel(sz_ref, x_ref, b_ref, c_ref, tab_ref, dcol_ref, gcol_ref,
                     *rest, seq_chunks, has_h0, has_prev, emit_state, unroll):
    rest = list(rest)
    h0_ref = rest.pop(0) if has_h0 else None
    prev_ref = rest.pop(0) if has_prev else None
    y_ref = rest.pop(0)
    st_ref = rest.pop(0) if emit_state else None
    yp_s, stf_s, stb_s, hf_s, hb_s, seg_s, cb_s = rest

    nc, gw, _ = x_ref.shape
    n = b_ref.shape[1]
    hp = gw // HEADS_PER_GROUP
    nh = HEADS_PER_GROUP
    r2 = 2 * nh
    assert not has_h0 or nc == seq_chunks

    li = lax.broadcasted_iota(jnp.int32, (CHUNK, CHUNK), 0)
    si = lax.broadcasted_iota(jnp.int32, (CHUNK, CHUNK), 1)

    def expand(rows4):
        return jnp.concatenate(
            [jnp.broadcast_to(rows4[h:h + 1, :], (hp, LANES)) for h in range(nh)], axis=0)

    blk = (lax.broadcasted_iota(jnp.int32, (r2, r2 * CHUNK), 1) // CHUNK
           == lax.broadcasted_iota(jnp.int32, (r2, r2 * CHUNK), 0))
    sel_k = jnp.concatenate([jnp.where(blk, 1.0, 0.0)] * 3, axis=0)
    ones_k = jnp.ones((3 * r2, CHUNK), F32)

    def split3(v):
        p0 = v.astype(BF16).astype(F32)
        p1 = (v - p0).astype(BF16).astype(F32)
        return p0, p1, v - p0 - p1

    def seg_chunk(c, carry):
        lhs = jnp.concatenate(list(split3(tab_ref[c, TAB_CUM2])) + [ones_k], axis=0)
        rhs = jnp.concatenate(
            [sel_k] + [jnp.where(blk, -jnp.concatenate([p] * r2, axis=1), 0.0)
                       for p in split3(tab_ref[c, TAB_ROW2])], axis=0)
        seg_s[c] = lax.dot_general(lhs.astype(BF16), rhs.astype(BF16), (((0,), (0,)), ((), ())),
                                   preferred_element_type=F32)
        cb_s[c] = lax.dot_general(c_ref[c], b_ref[c], (((0,), (0,)), ((), ())),
                                  preferred_element_type=F32)
        return carry

    lax.fori_loop(0, nc, seg_chunk, 0, unroll=unroll)

    def local_chunk(c, carry):
        xst_b = x_ref[c]
        xst = xst_b.astype(F32)
        bm_t = b_ref[c]
        cb = cb_s[c]
        parts = []
        for h in range(nh):
            hb = nh + h
            arg = jnp.where(si <= li, seg_s[c, :, h * CHUNK:(h + 1) * CHUNK],
                            seg_s[c, :, hb * CHUNK:(hb + 1) * CHUNK])
            w = (cb * jnp.exp2(arg)).astype(BF16)
            parts.append(_dot_nt(xst_b[h * hp:(h + 1) * hp, :], w))
        cb_diag = jnp.sum(jnp.where(si == li, cb, 0.0), axis=0, keepdims=True)
        skip = dcol_ref[...] + expand(tab_ref[c, TAB_DT][nh:r2] * cb_diag)
        yp_s[c] = jnp.concatenate(parts, axis=0) + skip * xst

        coef8 = tab_ref[c, TAB_COEF]
        xdw = jnp.concatenate([xst * expand(coef8[0:nh]), xst * expand(coef8[nh:r2])], axis=0)
        st = _dot_nt(xdw.astype(BF16), bm_t)
        stf_s[c] = st[0:gw]
        stb_s[c] = st[gw:2 * gw]
        return carry

    lax.fori_loop(0, nc, local_chunk, 0, unroll=unroll)

    if has_h0:
        h0f = h0_ref[0].reshape(gw, n)
        h0b = h0_ref[1].reshape(gw, n)
    else:
        h0f = jnp.zeros((gw, n), F32)
        h0b = h0f
    if has_prev:
        st_ref[:, 0:prev_ref.shape[1]] = prev_ref[...]
    slot = st_ref.shape[1] - 1 if emit_state else None

    def fwd_state(c, h):
        if nc > seq_chunks:
            h = jnp.where(c % seq_chunks == 0, 0.0, h)
        hf_s[c] = h.astype(BF16)
        h = h * expand(tab_ref[c, TAB_CD][0:nh]) + stf_s[c]
        if emit_state:
            @pl.when(c % seq_chunks == seq_chunks - 1)
            def _():
                st_ref[c // seq_chunks, slot, 0] = h.reshape(nh, hp, n)
        return h

    def bwd_state(i, h):
        c = nc - 1 - i
        if nc > seq_chunks:
            h = jnp.where(c % seq_chunks == seq_chunks - 1, 0.0, h)
        hb_s[c] = h.astype(BF16)
        h = h * expand(tab_ref[c, TAB_CD][nh:r2]) + stb_s[c]
        if emit_state:
            @pl.when(c % seq_chunks == 0)
            def _():
                st_ref[c // seq_chunks, slot, 1] = h.reshape(nh, hp, n)
        return h

    lax.fori_loop(0, nc, fwd_state, h0f)
    lax.fori_loop(0, nc, bwd_state, h0b)

    def output_chunk(c, carry):
        hin = jnp.concatenate([hf_s[c], hb_s[c]], axis=0)
        yo = _dot(hin, c_ref[c])
        e8 = tab_ref[c, TAB_E]
        y_t = yp_s[c] + yo[0:gw] * expand(e8[0:nh]) + yo[gw:2 * gw] * expand(e8[nh:r2])
        y = y_t * sz_ref[c].astype(F32)
        inv = lax.rsqrt(jnp.mean(y * y, axis=0, keepdims=True) + EPS)
        y_ref[c] = (y * inv * gcol_ref[...]).astype(BF16)
        return carry

    lax.fori_loop(0, nc, output_chunk, 0, unroll=unroll)


def _ssd_out_kernel(y_ref, w_ref, x_ref, mod_ref, *rest, final):
    fg_ref = rest[0] if final else None
    o_ref = rest[-1]
    y = jnp.concatenate([y_ref[k].T for k in range(y_ref.shape[0])], axis=0)
    out = _dot(y, w_ref[...])
    o_ref[...] = _residual(x_ref[...], out, mod_ref, fg_ref)


def _ssd_layer(x2d, seq, mod3, mod_row, norm_g, w_zxbc, wdt_t, dt_bias_col, alog_col,
               conv_wb, d_col, gn_g, w_out, h0, prev_state, layer_idx, emit_state, final_g):
    t, d = x2d.shape
    d_inner = w_out.shape[1]
    heads2 = wdt_t.shape[1]
    groups = heads2 // (2 * HEADS_PER_GROUP)
    gw = d_inner // groups
    n = (w_zxbc.shape[1] - 2 * d_inner) // (2 * groups)
    hp = gw // HEADS_PER_GROUP
    ln = seq.length
    nc = ln // CHUNK
    tm = min(512, t)
    tiles_per_seq = max(ln // tm, 1)
    row_of_tile = lambda i: mod_row(i // tiles_per_seq)
    const = lambda *_: (0, 0)
    tok = pl.BlockSpec((tm, d), lambda i: (i, 0))
    wrows = w_zxbc.shape[1]

    nt = ln * max(1, min(SSD_IN_TOKENS, t) // ln) if seq.shared_cond else ln
    seqs_per_block = nt // ln
    rows_blk = min(SSD_IN_ROWS, d_inner)
    assert t % nt == 0 and d_inner % rows_blk == 0 and wrows % rows_blk == 0
    n_tab = TAB_CD + 1
    gate_blocks = d_inner // rows_blk
    proj, tab = pl.pallas_call(
        functools.partial(_ssd_in_kernel, seq_len=ln, gate_blocks=gate_blocks,
                          sub_rows=min(SSD_IN_SUB_ROWS, rows_blk)),
        grid=(t // nt, wrows // rows_blk),
        in_specs=[
            pl.BlockSpec((nt, d), lambda i, j: (i, 0), pipeline_mode=pl.Buffered(1)),
            pl.BlockSpec((1, d), const),
            pl.BlockSpec((None, 1, 3 * d), lambda i, j: (mod_row(i * seqs_per_block), 0, 0)),
            pl.BlockSpec((None, rows_blk, d), lambda i, j: (layer_idx, j, 0)),
            pl.BlockSpec((None, heads2, d), lambda i, j: (layer_idx, 0, 0)),
            pl.BlockSpec((heads2, 1), const),
            pl.BlockSpec((heads2, 1), const),
            pl.BlockSpec((rows_blk, 4), lambda i, j: (jnp.maximum(j - gate_blocks, 0), 0)),
        ],
        out_specs=[
            pl.BlockSpec((nt // CHUNK, rows_blk, CHUNK), lambda i, j: (i, j, 0)),
            pl.BlockSpec((nt // CHUNK, n_tab, heads2, CHUNK), lambda i, j: (i, 0, 0, 0)),
        ],
        out_shape=[
            jax.ShapeDtypeStruct((t // CHUNK, wrows, CHUNK), BF16),
            jax.ShapeDtypeStruct((t // CHUNK, n_tab, heads2, CHUNK), F32),
        ],
        scratch_shapes=[pltpu.VMEM((nt, d), BF16)],
        compiler_params=_params(2),
        name="ssd_in",
    )(x2d, norm_g[None], mod3, w_zxbc, wdt_t, dt_bias_col, alog_col, conv_wb)

    r2 = 2 * HEADS_PER_GROUP
    xg0 = d_inner // gw
    bg0 = 2 * d_inner // n
    cg0 = bg0 + groups
    has_h0 = h0 is not None
    has_prev = emit_state and prev_state is not None
    spc = 1 if has_h0 else max(1, min(seq.count, SCAN_CELL_CHUNKS // nc))
    assert seq.count % spc == 0
    cc = spc * nc
    nh = HEADS_PER_GROUP
    in_specs = [
        pl.BlockSpec((cc, gw, CHUNK), lambda s, g: (s, g, 0)),
        pl.BlockSpec((cc, gw, CHUNK), lambda s, g: (s, xg0 + g, 0)),
        pl.BlockSpec((cc, n, CHUNK), lambda s, g: (s, bg0 + g, 0)),
        pl.BlockSpec((cc, n, CHUNK), lambda s, g: (s, cg0 + g, 0)),
        pl.BlockSpec((cc, n_tab, r2, CHUNK), lambda s, g: (s, 0, g, 0)),
        pl.BlockSpec((gw, LANES), lambda s, g: (g, 0)),
        pl.BlockSpec((gw, LANES), lambda s, g: (g, 0)),
    ]
    args = [proj, proj, proj, proj, tab, jnp.broadcast_to(d_col, (d_inner, LANES)),
            jnp.broadcast_to(gn_g[:, None], (d_inner, LANES))]
    if has_h0:
        in_specs.append(pl.BlockSpec((None, None, 2, nh, hp, n),
                                     lambda s, g: (s, layer_idx, 0, g, 0, 0)))
        args.append(h0)
    if has_prev:
        in_specs.append(pl.BlockSpec((spc, prev_state.shape[1], 2, nh, hp, n),
                                     lambda s, g: (s, 0, 0, g, 0, 0)))
        args.append(prev_state)
    out_specs = [pl.BlockSpec((cc, gw, CHUNK), lambda s, g: (s, g, 0))]
    out_shape = [jax.ShapeDtypeStruct((t // CHUNK, d_inner, CHUNK), BF16)]
    if emit_state:
        slots = (prev_state.shape[1] if has_prev else 0) + 1
        out_specs.append(pl.BlockSpec((spc, slots, 2, nh, hp, n), lambda s, g: (s, 0, 0, g, 0, 0)))
        out_shape.append(jax.ShapeDtypeStruct((seq.count, slots, 2, groups * nh, hp, n), F32))
    res = pl.pallas_call(
        functools.partial(_ssd_scan_kernel, seq_chunks=nc, has_h0=has_h0, has_prev=has_prev,
                          emit_state=emit_state, unroll=min(4, cc)),
        grid=(seq.count // spc, groups),
        in_specs=in_specs,
        out_specs=out_specs,
        out_shape=out_shape,
        scratch_shapes=[
            pltpu.VMEM((cc, gw, CHUNK), F32),
            pltpu.VMEM((cc, gw, n), F32),
            pltpu.VMEM((cc, gw, n), F32),
            pltpu.VMEM((cc, gw, n), BF16),
            pltpu.VMEM((cc, gw, n), BF16),
            pltpu.VMEM((cc, CHUNK, r2 * CHUNK), F32),
            pltpu.VMEM((cc, CHUNK, CHUNK), F32),
        ],
        compiler_params=_params(2),
        name="ssd_scan",
    )(*args)
    y, state = (res[0], res[1]) if emit_state else (res[0], None)

    final = final_g is not None
    in_specs = [
        pl.BlockSpec((tm // CHUNK, d_inner, CHUNK), lambda i: (i, 0, 0)),
        pl.BlockSpec((None, d_inner, d), lambda i: (layer_idx, 0, 0), pipeline_mode=pl.Buffered(1)),
        tok,
        pl.BlockSpec((None, 1, 3 * d), lambda i: (row_of_tile(i), 0, 0)),
    ]
    args = [y, w_out, x2d, mod3]
    if final:
        in_specs.append(pl.BlockSpec((1, d), const))
        args.append(final_g[None])
    x_new = pl.pallas_call(
        functools.partial(_ssd_out_kernel, final=final),
        grid=(t // tm,),
        in_specs=in_specs,
        out_specs=tok,
        out_shape=jax.ShapeDtypeStruct((t, d), F32),
        compiler_params=_params(1),
        name="ssd_out",
    )(*args)
    return x_new, state


class _Seqs:
    def __init__(self, count, length, shared_cond):
        self.count = count
        self.length = length
        self.shared_cond = shared_cond


def _trunk(x, mod, mod_row, shared_cond, row_len, h0, emit_state, weights):
    (norm_g, fc_w_in, fc_conv_w, fc_w_out, ssd_w, ssd_wdt_t, ssd_dtb, ssd_alog, ssd_conv_wb,
     ssd_dcol, ssd_norm_g, ssd_w_out, final_norm_g) = weights
    bsz, ln, d = x.shape
    seq = _Seqs(bsz, ln, shared_cond)
    x2d = x.reshape(bsz * ln, d)
    depth = norm_g.shape[0]
    states = None
    for layer in range(depth):
        i = layer // 2
        mod3 = mod[layer].reshape(COND_ROWS, 1, 3 * d)
        final_g = final_norm_g if layer == depth - 1 else None
        if layer % 2 == 0:
            x2d = _fc_layer(x2d, seq, mod3, mod_row, norm_g[layer], fc_w_in, fc_conv_w[i],
                            fc_w_out, i, row_len, final_g)
        else:
            x2d, st = _ssd_layer(x2d, seq, mod3, mod_row, norm_g[layer], ssd_w, ssd_wdt_t,
                                 ssd_dtb[i], ssd_alog[i], ssd_conv_wb[i], ssd_dcol[i],
                                 ssd_norm_g[i], ssd_w_out, h0, states, i, emit_state, final_g)
            states = st
    return x2d.reshape(bsz, ln, d), states


def kernel(x_prompt, x_sample, state_ssm, c, c_ctx, w_mod, b_mod, norm_g, fc_w_in, fc_conv_w,
           fc_w_out, ssd_w_in, ssd_conv_w, ssd_conv_b, ssd_dt_bias, ssd_a_log, ssd_d,
           ssd_norm_g, ssd_w_out, final_norm_g):
    d = x_prompt.shape[-1]
    dec_batch = x_sample.shape[0]
    no, _, heads = ssd_a_log.shape
    d_inner = ssd_w_out.shape[1]
    conv_dim = ssd_conv_w.shape[-1]
    hp = d_inner // heads
    groups = heads // HEADS_PER_GROUP
    assert dec_batch + 1 <= COND_ROWS and heads % HEADS_PER_GROUP == 0

    cond = jnp.zeros((COND_ROWS, d), F32).at[:dec_batch].set(c).at[dec_batch].set(c_ctx)
    mod = _modulation(cond, w_mod, b_mod)

    zx = d_inner + conv_dim
    ssd_w = jnp.swapaxes(ssd_w_in[:, :, :zx], 1, 2).astype(BF16)
    ssd_conv_wb = jnp.swapaxes(jnp.concatenate([ssd_conv_w, ssd_conv_b[:, None]], axis=1), 1, 2)
    perm = jnp.arange(2 * heads).reshape(2, groups, HEADS_PER_GROUP).transpose(1, 0, 2).reshape(-1)
    ssd_wdt_t = jnp.swapaxes(ssd_w_in[:, :, zx:][:, :, perm], 1, 2).astype(BF16)
    ssd_dtb = ssd_dt_bias.reshape(no, 2 * heads)[:, perm][:, :, None]
    ssd_alog = ssd_a_log.reshape(no, 2 * heads)[:, perm][:, :, None]
    ssd_dcol = jnp.repeat(ssd_d, hp, axis=1)[:, :, None]
    weights = (norm_g, fc_w_in.astype(BF16), fc_conv_w, fc_w_out.astype(BF16), ssd_w, ssd_wdt_t,
               ssd_dtb, ssd_alog, ssd_conv_wb, ssd_dcol, ssd_norm_g,
               ssd_w_out.astype(BF16), final_norm_g)

    y_prompt, states = _trunk(x_prompt, mod, lambda s: dec_batch, True, x_prompt.shape[1], None,
                              True, weights)
    y_sample, _ = _trunk(x_sample, mod, lambda s: s, False, GRID_W, state_ssm, False, weights)
    return y_prompt, y_sample, states
```

```python
import functools
import math

import jax
import jax.numpy as jnp
from jax import lax
from jax.experimental import pallas as pl
from jax.experimental.pallas import tpu as pltpu

F32 = jnp.float32
BF16 = jnp.bfloat16

EPS = 1e-6
GRID_W = 64
FOURIER_GROUPS = 4
DFT_INNER = 64
CHUNK = 128
HEADS_PER_GROUP = 4
LANES = 128
SSD_IN_TOKENS = 2048
SSD_IN_ROWS = 1024
SSD_IN_SUB_ROWS = 256
SCAN_CELL_CHUNKS = 8
COND_ROWS = 16
VMEM_LIMIT = 56 * 1024 * 1024


def _params(n_grid):
    return pltpu.CompilerParams(
        dimension_semantics=("arbitrary",) * n_grid, vmem_limit_bytes=VMEM_LIMIT)


def _dot(a, b):
    return jnp.dot(a, b, preferred_element_type=F32)


def _dot_nt(a, b):
    return lax.dot_general(a, b, (((1,), (1,)), ((), ())), preferred_element_type=F32)


def _sigmoid(x):
    return 1.0 / (1.0 + jnp.exp(-x))


def _silu(x):
    return x * _sigmoid(x)


def _rms(x):
    return x * lax.rsqrt(jnp.mean(x * x, axis=-1, keepdims=True) + EPS)


def _modnorm(x, g_ref, mod_ref):
    d = x.shape[-1]
    shift = mod_ref[:, 0:d]
    scale = mod_ref[:, d:2 * d]
    return _rms(x) * g_ref[...] * (1.0 + scale) + shift


def _residual(x, out, mod_ref, fg_ref):
    d = x.shape[-1]
    xn = x + mod_ref[:, 2 * d:3 * d] * out
    if fg_ref is not None:
        xn = _rms(xn) * fg_ref[...]
    return xn


def _mod_kernel(c_ref, w_ref, b_ref, o_ref):
    act = _silu(c_ref[...]).astype(BF16)
    o_ref[...] = _dot(act, w_ref[...].astype(BF16)) + b_ref[...]


def _modulation(cond, w_mod, b_mod):
    depth, d, d3 = w_mod.shape
    nb = d3 // d
    return pl.pallas_call(
        _mod_kernel,
        grid=(depth, nb),
        in_specs=[
            pl.BlockSpec((COND_ROWS, d), lambda l, j: (0, 0)),
            pl.BlockSpec((None, d, d), lambda l, j: (l, 0, j)),
            pl.BlockSpec((None, 1, d), lambda l, j: (l, 0, j)),
        ],
        out_specs=pl.BlockSpec((None, COND_ROWS, d), lambda l, j: (l, 0, j)),
        out_shape=jax.ShapeDtypeStruct((depth, COND_ROWS, d3), F32),
        compiler_params=_params(2),
        name="modulation",
    )(cond, w_mod, b_mod.reshape(depth, 1, d3))


def _fc_in_kernel(x_ref, g_ref, mod_ref, w_ref, cw_ref, dft_ref,
                  xc_ref, xs_ref, ga_ref, yb_ref, *, row_len, col_block):
    tm, d = x_ref.shape
    gd = d // FOURIER_GROUPS
    hm = _modnorm(x_ref[...], g_ref, mod_ref).astype(BF16)
    ua = _dot(hm, w_ref[:, 0:d]).astype(BF16)
    for g in range(FOURIER_GROUPS):
        t = _dot(ua[:, g * gd:(g + 1) * gd], dft_ref[...])
        xc_ref[:, g * gd:(g + 1) * gd] = t[:, :gd].astype(BF16)
        xs_ref[:, g * gd:(g + 1) * gd] = t[:, gd:].astype(BF16)
    pos = lax.rem(lax.broadcasted_iota(jnp.int32, (tm, 1), 0), row_len)
    first = pos == 0
    last = pos == row_len - 1
    for j in range(d // col_block):
        lo = j * col_block
        za = _dot(hm, w_ref[:, d + lo:d + lo + col_block])
        ga_ref[:, lo:lo + col_block] = _silu(za).astype(BF16)
        bb = _dot(hm, w_ref[:, 2 * d + lo:2 * d + lo + col_block])
        cc = _dot(hm, w_ref[:, 3 * d + lo:3 * d + lo + col_block])
        vv = _dot(hm, w_ref[:, 4 * d + lo:4 * d + lo + col_block])
        zb = _dot(hm, w_ref[:, 5 * d + lo:5 * d + lo + col_block])
        u = cc * vv
        up = jnp.where(first, 0.0, pltpu.roll(u, 1, 0))
        un = jnp.where(last, 0.0, pltpu.roll(u, tm - 1, 0))
        cw = cw_ref[:, lo:lo + col_block]
        y = up * cw[0:1] + u * cw[1:2] + un * cw[2:3]
        yb_ref[:, lo:lo + col_block] = (bb * y * _silu(zb)).astype(BF16)


def _fc_out_kernel(dc_ref, ds_ref, xc_ref, xs_ref, ga_ref, yb_ref, w_ref, x_ref, mod_ref,
                   *rest, final):
    fg_ref = rest[0] if final else None
    o_ref = rest[-1]
    d = x_ref.shape[1]
    ya = _dot(dc_ref[...], xc_ref[...]) + _dot(ds_ref[...], xs_ref[...])
    h1 = (ya * ga_ref[...].astype(F32)).astype(BF16)
    out = _dot(h1, w_ref[0:d, :]) + _dot(yb_ref[...], w_ref[d:2 * d, :])
    o_ref[...] = _residual(x_ref[...], out, mod_ref, fg_ref)


def _dft_tables(n, dtype=BF16):
    nb = DFT_INNER if n % DFT_INNER == 0 else 1
    k = jnp.arange(n, dtype=jnp.int32)[:, None]

    def base(cols):
        ang = ((k * cols[None, :]) % n).astype(F32) * (2.0 * math.pi / n)
        return jnp.cos(ang), jnp.sin(ang)

    ca, sa = base(jnp.arange(n // nb, dtype=jnp.int32) * nb)
    cb, sb = base(jnp.arange(nb, dtype=jnp.int32))
    ca, sa, cb, sb = ca[:, :, None], sa[:, :, None], cb[:, None, :], sb[:, None, :]
    s = 1.0 / math.sqrt(n)
    cos = ((ca * cb - sa * sb) * s).reshape(n, n)
    sin = ((sa * cb + ca * sb) * s).reshape(n, n)
    return cos.astype(dtype), sin.astype(dtype)


def _fc_layer(x2d, seq, mod3, mod_row, norm_g, w_in, conv_w, w_out, li, row_len, final_g):
    t, d = x2d.shape
    tm = min(512, t)
    tiles_per_seq = max(seq.length // tm, 1)
    gd = d // FOURIER_GROUPS
    cc, sc = _dft_tables(gd)
    dft_ch = jnp.concatenate([cc, sc], axis=1)
    row_of_tile = lambda i: mod_row(i // tiles_per_seq)
    const = lambda *_: (0, 0)
    tok = pl.BlockSpec((tm, d), lambda i: (i, 0))
    xc, xs, ga, yb = pl.pallas_call(
        functools.partial(_fc_in_kernel, row_len=row_len, col_block=min(256, d)),
        grid=(t // tm,),
        in_specs=[
            tok,
            pl.BlockSpec((1, d), const),
            pl.BlockSpec((None, 1, 3 * d), lambda i: (row_of_tile(i), 0, 0)),
            pl.BlockSpec((None, d, 6 * d), lambda i: (li, 0, 0), pipeline_mode=pl.Buffered(1)),
            pl.BlockSpec((3, d), const),
            pl.BlockSpec((gd, 2 * gd), const),
        ],
        out_specs=[tok, tok, tok, tok],
        out_shape=[jax.ShapeDtypeStruct((t, d), BF16)] * 4,
        compiler_params=_params(1),
        name="fc_in",
    )(x2d, norm_g[None], mod3, w_in, conv_w, dft_ch)

    ln = seq.length
    tr = min(512, ln)
    rt = ln // tr
    cl, sl = _dft_tables(ln)
    final = final_g is not None
    row = pl.BlockSpec((tr, d), lambda s, r: (s * rt + r, 0))
    whole = pl.BlockSpec((ln, d), lambda s, r: (s, 0))
    in_specs = [
        pl.BlockSpec((tr, ln), lambda s, r: (r, 0)),
        pl.BlockSpec((tr, ln), lambda s, r: (r, 0)),
        whole, whole, row, row,
        pl.BlockSpec((None, 2 * d, d), lambda s, r: (li, 0, 0), pipeline_mode=pl.Buffered(1)),
        row,
        pl.BlockSpec((None, 1, 3 * d), lambda s, r: (mod_row(s), 0, 0)),
    ]
    args = [cl, -sl, xc, xs, ga, yb, w_out, x2d, mod3]
    if final:
        in_specs.append(pl.BlockSpec((1, d), lambda s, r: (0, 0)))
        args.append(final_g[None])
    return pl.pallas_call(
        functools.partial(_fc_out_kernel, final=final),
        grid=(seq.count, rt),
        in_specs=in_specs,
        out_specs=row,
        out_shape=jax.ShapeDtypeStruct((t, d), F32),
        compiler_params=_params(2),
        name="fc_out",
    )(*args)


TAB_DT, TAB_CUM2, TAB_ROW2, TAB_COEF, TAB_E, TAB_CD = range(6)
LOG2_E = math.log2(math.e)
LOG2_FLOOR = -1e30


def _ssd_in_kernel(x_ref, g_ref, mod_ref, wt_ref, wdt_ref, dtb_ref, alog_ref, cw_ref,
                   proj_ref, tab_ref, hm_s, *, seq_len, gate_blocks, sub_rows):
    j = pl.program_id(1)
    nt = x_ref.shape[0]
    rows = wt_ref.shape[0]
    nc = nt // CHUNK

    def emit(r0, y):
        yb = y.astype(BF16)
        for k in range(nc):
            proj_ref[k, pl.ds(r0, sub_rows), :] = yb[:, k * CHUNK:(k + 1) * CHUNK]

    @pl.when(j == 0)
    def _():
        hm = _modnorm(x_ref[...], g_ref, mod_ref).astype(BF16)
        hm_s[...] = hm
        _scan_tables(hm, wdt_ref, dtb_ref, alog_ref, tab_ref)

    @pl.when(j < gate_blocks)
    def _():
        for r0 in range(0, rows, sub_rows):
            emit(r0, _silu(_dot_nt(wt_ref[r0:r0 + sub_rows, :], hm_s[...])))

    @pl.when(j >= gate_blocks)
    def _():
        lane0 = lax.broadcasted_iota(jnp.int32, (sub_rows, LANES), 1)

        @pl.loop(0, rows // sub_rows)
        def _(i):
            r0 = pl.multiple_of(i * sub_rows, sub_rows)
            p = _dot_nt(wt_ref[pl.ds(r0, sub_rows), :], hm_s[...])
            left = pltpu.roll(p, 1, 1)
            right = pltpu.roll(p, nt - 1, 1)
            lcols = [left[:, q:q + LANES] for q in range(0, nt, LANES)]
            rcols = [right[:, q:q + LANES] for q in range(0, nt, LANES)]
            for q in range(0, nt, seq_len):
                lcols[q // LANES] = jnp.where(lane0 == 0, 0.0, lcols[q // LANES])
                e = (q + seq_len) // LANES - 1
                rcols[e] = jnp.where(lane0 == LANES - 1, 0.0, rcols[e])
            left = jnp.concatenate(lcols, axis=1)
            right = jnp.concatenate(rcols, axis=1)
            cw = cw_ref[pl.ds(r0, sub_rows), :]
            y = left * cw[:, 0:1] + p * cw[:, 1:2] + right * cw[:, 2:3] + cw[:, 3:4]
            emit(r0, _silu(y))


def _scan_tables(hm, wdt_ref, dtb_ref, alog_ref, tab_ref):
    v = _dot_nt(wdt_ref[...], hm) + dtb_ref[...]
    dt_all = jnp.maximum(v, 0.0) + jnp.log(1.0 + jnp.exp(-jnp.abs(v)))
    heads2, nt = dt_all.shape
    nc = nt // CHUNK
    la_all = dt_all * -jnp.exp(alog_ref[...])
    dt = jnp.concatenate([dt_all[:, k * CHUNK:(k + 1) * CHUNK] for k in range(nc)], axis=0)
    la = jnp.concatenate([la_all[:, k * CHUNK:(k + 1) * CHUNK] for k in range(nc)], axis=0)
    p0 = la.astype(BF16)
    r1 = la - p0.astype(F32)
    p1 = r1.astype(BF16)
    p2 = (r1 - p1.astype(F32)).astype(BF16)
    ji = lax.broadcasted_iota(jnp.int32, (CHUNK, CHUNK), 0)
    li = lax.broadcasted_iota(jnp.int32, (CHUNK, CHUNK), 1)
    upto = jnp.where(ji <= li, 1.0, 0.0).astype(BF16)
    from_ = jnp.where(ji >= li, 1.0, 0.0).astype(BF16)
    pre = _dot(p0, upto) + _dot(p1, upto) + _dot(p2, upto)
    suf = _dot(p0, from_) + _dot(p1, from_) + _dot(p2, from_)
    row = lax.broadcasted_iota(jnp.int32, la.shape, 0)
    is_fwd = (row & (2 * HEADS_PER_GROUP - 1)) < HEADS_PER_GROUP
    cum = jnp.where(is_fwd, pre, suf)
    tot = jnp.where(is_fwd, jnp.broadcast_to(cum[:, CHUNK - 1:CHUNK], cum.shape),
                    jnp.broadcast_to(cum[:, 0:1], cum.shape))
    coef = dt * jnp.exp(tot - cum)
    e = jnp.exp(cum)
    cd = jnp.exp(tot)
    cum2 = cum * LOG2_E
    row2 = cum2 - jnp.maximum(jnp.log2(dt), LOG2_FLOOR)
    for k in range(nc):
        rows = slice(k * heads2, (k + 1) * heads2)
        tab_ref[k, TAB_DT] = dt[rows]
        tab_ref[k, TAB_CUM2] = cum2[rows]
        tab_ref[k, TAB_ROW2] = row2[rows]
        tab_ref[k, TAB_COEF] = coef[rows]
        tab_ref[k, TAB_E] = e[rows]
        tab_ref[k, TAB_CD] = cd[rows]


def _ssd_scan_kernel(sz_ref, x_ref, b_ref, c_ref, tab_ref, dcol_ref, gcol_ref,
                     *rest, seq_chunks, has_h0, has_prev, emit_state, unroll):
    rest = list(rest)
    h0_ref = rest.pop(0) if has_h0 else None
    prev_ref = rest.pop(0) if has_prev else None
    y_ref = rest.pop(0)
    st_ref = rest.pop(0) if emit_state else None
    yp_s, stf_s, stb_s, hf_s, hb_s, seg_s, cb_s = rest

    nc, gw, _ = x_ref.shape
    n = b_ref.shape[1]
    hp = gw // HEADS_PER_GROUP
    nh = HEADS_PER_GROUP
    r2 = 2 * nh
    assert not has_h0 or nc == seq_chunks

    li = lax.broadcasted_iota(jnp.int32, (CHUNK, CHUNK), 0)
    si = lax.broadcasted_iota(jnp.int32, (CHUNK, CHUNK), 1)

    def expand(rows4):
        return jnp.concatenate(
            [jnp.broadcast_to(rows4[h:h + 1, :], (hp, LANES)) for h in range(nh)], axis=0)

    blk = (lax.broadcasted_iota(jnp.int32, (r2, r2 * CHUNK), 1) // CHUNK
           == lax.broadcasted_iota(jnp.int32, (r2, r2 * CHUNK), 0))
    sel_k = jnp.concatenate([jnp.where(blk, 1.0, 0.0)] * 3, axis=0)
    ones_k = jnp.ones((3 * r2, CHUNK), F32)

    def split3(v):
        p0 = v.astype(BF16).astype(F32)
        p1 = (v - p0).astype(BF16).astype(F32)
        return p0, p1, v - p0 - p1

    def seg_chunk(c, carry):
        lhs = jnp.concatenate(list(split3(tab_ref[c, TAB_CUM2])) + [ones_k], axis=0)
        rhs = jnp.concatenate(
            [sel_k] + [jnp.where(blk, -jnp.concatenate([p] * r2, axis=1), 0.0)
                       for p in split3(tab_ref[c, TAB_ROW2])], axis=0)
        seg_s[c] = lax.dot_general(lhs.astype(BF16), rhs.astype(BF16), (((0,), (0,)), ((), ())),
                                   preferred_element_type=F32)
        cb_s[c] = lax.dot_general(c_ref[c], b_ref[c], (((0,), (0,)), ((), ())),
                                  preferred_element_type=F32)
        return carry

    lax.fori_loop(0, nc, seg_chunk, 0, unroll=unroll)

    def local_chunk(c, carry):
        xst_b = x_ref[c]
        xst = xst_b.astype(F32)
        bm_t = b_ref[c]
        cb = cb_s[c]
        parts = []
        for h in range(nh):
            hb = nh + h
            arg = jnp.where(si <= li, seg_s[c, :, h * CHUNK:(h + 1) * CHUNK],
                            seg_s[c, :, hb * CHUNK:(hb + 1) * CHUNK])
            w = (cb * jnp.exp2(arg)).astype(BF16)
            parts.append(_dot_nt(xst_b[h * hp:(h + 1) * hp, :], w))
        cb_diag = jnp.sum(jnp.where(si == li, cb, 0.0), axis=0, keepdims=True)
        skip = dcol_ref[...] + expand(tab_ref[c, TAB_DT][nh:r2] * cb_diag)
        yp_s[c] = jnp.concatenate(parts, axis=0) + skip * xst

        coef8 = tab_ref[c, TAB_COEF]
        xdw = jnp.concatenate([xst * expand(coef8[0:nh]), xst * expand(coef8[nh:r2])], axis=0)
        st = _dot_nt(xdw.astype(BF16), bm_t)
        stf_s[c] = st[0:gw]
        stb_s[c] = st[gw:2 * gw]
        return carry

    lax.fori_loop(0, nc, local_chunk, 0, unroll=unroll)

    if has_h0:
        h0f = h0_ref[0].reshape(gw, n)
        h0b = h0_ref[1].reshape(gw, n)
    else:
        h0f = jnp.zeros((gw, n), F32)
        h0b = h0f
    if has_prev:
        st_ref[:, 0:prev_ref.shape[1]] = prev_ref[...]
    slot = st_ref.shape[1] - 1 if emit_state else None

    def fwd_state(c, h):
        if nc > seq_chunks:
            h = jnp.where(c % seq_chunks == 0, 0.0, h)
        hf_s[c] = h.astype(BF16)
        h = h * expand(tab_ref[c, TAB_CD][0:nh]) + stf_s[c]
        if emit_state:
            @pl.when(c % seq_chunks == seq_chunks - 1)
            def _():
                st_ref[c // seq_chunks, slot, 0] = h.reshape(nh, hp, n)
        return h

    def bwd_state(i, h):
        c = nc - 1 - i
        if nc > seq_chunks:
            h = jnp.where(c % seq_chunks == seq_chunks - 1, 0.0, h)
        hb_s[c] = h.astype(BF16)
        h = h * expand(tab_ref[c, TAB_CD][nh:r2]) + stb_s[c]
        if emit_state:
            @pl.when(c % seq_chunks == 0)
            def _():
                st_ref[c // seq_chunks, slot, 1] = h.reshape(nh, hp, n)
        return h

    lax.fori_loop(0, nc, fwd_state, h0f)
    lax.fori_loop(0, nc, bwd_state, h0b)

    def output_chunk(c, carry):
        hin = jnp.concatenate([hf_s[c], hb_s[c]], axis=0)
        yo = _dot(hin, c_ref[c])
        e8 = tab_ref[c, TAB_E]
        y_t = yp_s[c] + yo[0:gw] * expand(e8[0:nh]) + yo[gw:2 * gw] * expand(e8[nh:r2])
        y = y_t * sz_ref[c].astype(F32)
        inv = lax.rsqrt(jnp.mean(y * y, axis=0, keepdims=True) + EPS)
        y_ref[c] = (y * inv * gcol_ref[...]).astype(BF16)
        return carry

    lax.fori_loop(0, nc, output_chunk, 0, unroll=unroll)


def _ssd_out_kernel(y_ref, w_ref, x_ref, mod_ref, *rest, final):
    fg_ref = rest[0] if final else None
    o_ref = rest[-1]
    y = jnp.concatenate([y_ref[k].T for k in range(y_ref.shape[0])], axis=0)
    out = _dot(y, w_ref[...])
    o_ref[...] = _residual(x_ref[...], out, mod_ref, fg_ref)


def _ssd_layer(x2d, seq, mod3, mod_row, norm_g, w_zxbc, wdt_t, dt_bias_col, alog_col,
               conv_wb, d_col, gn_g, w_out, h0, prev_state, layer_idx, emit_state, final_g):
    t, d = x2d.shape
    d_inner = w_out.shape[1]
    heads2 = wdt_t.shape[1]
    groups = heads2 // (2 * HEADS_PER_GROUP)
    gw = d_inner // groups
    n = (w_zxbc.shape[1] - 2 * d_inner) // (2 * groups)
    hp = gw // HEADS_PER_GROUP
    ln = seq.length
    nc = ln // CHUNK
    tm = min(512, t)
    tiles_per_seq = max(ln // tm, 1)
    row_of_tile = lambda i: mod_row(i // tiles_per_seq)
    const = lambda *_: (0, 0)
    tok = pl.BlockSpec((tm, d), lambda i: (i, 0))
    wrows = w_zxbc.shape[1]

    nt = ln * max(1, min(SSD_IN_TOKENS, t) // ln) if seq.shared_cond else ln
    seqs_per_block = nt // ln
    rows_blk = min(SSD_IN_ROWS, d_inner)
    assert t % nt == 0 and d_inner % rows_blk == 0 and wrows % rows_blk == 0
    n_tab = TAB_CD + 1
    gate_blocks = d_inner // rows_blk
    proj, tab = pl.pallas_call(
        functools.partial(_ssd_in_kernel, seq_len=ln, gate_blocks=gate_blocks,
                          sub_rows=min(SSD_IN_SUB_ROWS, rows_blk)),
        grid=(t // nt, wrows // rows_blk),
        in_specs=[
            pl.BlockSpec((nt, d), lambda i, j: (i, 0), pipeline_mode=pl.Buffered(1)),
            pl.BlockSpec((1, d), const),
            pl.BlockSpec((None, 1, 3 * d), lambda i, j: (mod_row(i * seqs_per_block), 0, 0)),
            pl.BlockSpec((None, rows_blk, d), lambda i, j: (layer_idx, j, 0)),
            pl.BlockSpec((None, heads2, d), lambda i, j: (layer_idx, 0, 0)),
            pl.BlockSpec((heads2, 1), const),
            pl.BlockSpec((heads2, 1), const),
            pl.BlockSpec((rows_blk, 4), lambda i, j: (jnp.maximum(j - gate_blocks, 0), 0)),
        ],
        out_specs=[
            pl.BlockSpec((nt // CHUNK, rows_blk, CHUNK), lambda i, j: (i, j, 0)),
            pl.BlockSpec((nt // CHUNK, n_tab, heads2, CHUNK), lambda i, j: (i, 0, 0, 0)),
        ],
        out_shape=[
            jax.ShapeDtypeStruct((t // CHUNK, wrows, CHUNK), BF16),
            jax.ShapeDtypeStruct((t // CHUNK, n_tab, heads2, CHUNK), F32),
        ],
        scratch_shapes=[pltpu.VMEM((nt, d), BF16)],
        compiler_params=_params(2),
        name="ssd_in",
    )(x2d, norm_g[None], mod3, w_zxbc, wdt_t, dt_bias_col, alog_col, conv_wb)

    r2 = 2 * HEADS_PER_GROUP
    xg0 = d_inner // gw
    bg0 = 2 * d_inner // n
    cg0 = bg0 + groups
    has_h0 = h0 is not None
    has_prev = emit_state and prev_state is not None
    spc = 1 if has_h0 else max(1, min(seq.count, SCAN_CELL_CHUNKS // nc))
    assert seq.count % spc == 0
    cc = spc * nc
    nh = HEADS_PER_GROUP
    in_specs = [
        pl.BlockSpec((cc, gw, CHUNK), lambda s, g: (s, g, 0)),
        pl.BlockSpec((cc, gw, CHUNK), lambda s, g: (s, xg0 + g, 0)),
        pl.BlockSpec((cc, n, CHUNK), lambda s, g: (s, bg0 + g, 0)),
        pl.BlockSpec((cc, n, CHUNK), lambda s, g: (s, cg0 + g, 0)),
        pl.BlockSpec((cc, n_tab, r2, CHUNK), lambda s, g: (s, 0, g, 0)),
        pl.BlockSpec((gw, LANES), lambda s, g: (g, 0)),
        pl.BlockSpec((gw, LANES), lambda s, g: (g, 0)),
    ]
    args = [proj, proj, proj, proj, tab, jnp.broadcast_to(d_col, (d_inner, LANES)),
            jnp.broadcast_to(gn_g[:, None], (d_inner, LANES))]
    if has_h0:
        in_specs.append(pl.BlockSpec((None, None, 2, nh, hp, n),
                                     lambda s, g: (s, layer_idx, 0, g, 0, 0)))
        args.append(h0)
    if has_prev:
        in_specs.append(pl.BlockSpec((spc, prev_state.shape[1], 2, nh, hp, n),
                                     lambda s, g: (s, 0, 0, g, 0, 0)))
        args.append(prev_state)
    out_specs = [pl.BlockSpec((cc, gw, CHUNK), lambda s, g: (s, g, 0))]
    out_shape = [jax.ShapeDtypeStruct((t // CHUNK, d_inner, CHUNK), BF16)]
    if emit_state:
        slots = (prev_state.shape[1] if has_prev else 0) + 1
        out_specs.append(pl.BlockSpec((spc, slots, 2, nh, hp, n), lambda s, g: (s, 0, 0, g, 0, 0)))
        out_shape.append(jax.ShapeDtypeStruct((seq.count, slots, 2, groups * nh, hp, n), F32))
    res = pl.pallas_call(
        functools.partial(_ssd_scan_kernel, seq_chunks=nc, has_h0=has_h0, has_prev=has_prev,
                          emit_state=emit_state, unroll=min(16, cc)),
        grid=(seq.count // spc, groups),
        in_specs=in_specs,
        out_specs=out_specs,
        out_shape=out_shape,
        scratch_shapes=[
            pltpu.VMEM((cc, gw, CHUNK), F32),
            pltpu.VMEM((cc, gw, n), F32),
            pltpu.VMEM((cc, gw, n), F32),
            pltpu.VMEM((cc, gw, n), BF16),
            pltpu.VMEM((cc, gw, n), BF16),
            pltpu.VMEM((cc, CHUNK, r2 * CHUNK), F32),
            pltpu.VMEM((cc, CHUNK, CHUNK), F32),
        ],
        compiler_params=_params(2),
        name="ssd_scan",
    )(*args)
    y, state = (res[0], res[1]) if emit_state else (res[0], None)

    final = final_g is not None
    in_specs = [
        pl.BlockSpec((tm // CHUNK, d_inner, CHUNK), lambda i: (i, 0, 0)),
        pl.BlockSpec((None, d_inner, d), lambda i: (layer_idx, 0, 0), pipeline_mode=pl.Buffered(1)),
        tok,
        pl.BlockSpec((None, 1, 3 * d), lambda i: (row_of_tile(i), 0, 0)),
    ]
    args = [y, w_out, x2d, mod3]
    if final:
        in_specs.append(pl.BlockSpec((1, d), const))
        args.append(final_g[None])
    x_new = pl.pallas_call(
        functools.partial(_ssd_out_kernel, final=final),
        grid=(t // tm,),
        in_specs=in_specs,
        out_specs=tok,
        out_shape=jax.ShapeDtypeStruct((t, d), F32),
        compiler_params=_params(1),
        name="ssd_out",
    )(*args)
    return x_new, state


class _Seqs:
    def __init__(self, count, length, shared_cond):
        self.count = count
        self.length = length
        self.shared_cond = shared_cond


def _trunk(x, mod, mod_row, shared_cond, row_len, h0, emit_state, weights):
    (norm_g, fc_w_in, fc_conv_w, fc_w_out, ssd_w, ssd_wdt_t, ssd_dtb, ssd_alog, ssd_conv_wb,
     ssd_dcol, ssd_norm_g, ssd_w_out, final_norm_g) = weights
    bsz, ln, d = x.shape
    seq = _Seqs(bsz, ln, shared_cond)
    x2d = x.reshape(bsz * ln, d)
    depth = norm_g.shape[0]
    states = None
    for layer in range(depth):
        i = layer // 2
        mod3 = mod[layer].reshape(COND_ROWS, 1, 3 * d)
        final_g = final_norm_g if layer == depth - 1 else None
        if layer % 2 == 0:
            x2d = _fc_layer(x2d, seq, mod3, mod_row, norm_g[layer], fc_w_in, fc_conv_w[i],
                            fc_w_out, i, row_len, final_g)
        else:
            x2d, st = _ssd_layer(x2d, seq, mod3, mod_row, norm_g[layer], ssd_w, ssd_wdt_t,
                                 ssd_dtb[i], ssd_alog[i], ssd_conv_wb[i], ssd_dcol[i],
                                 ssd_norm_g[i], ssd_w_out, h0, states, i, emit_state, final_g)
            states = st
    return x2d.reshape(bsz, ln, d), states


def kernel(x_prompt, x_sample, state_ssm, c, c_ctx, w_mod, b_mod, norm_g, fc_w_in, fc_conv_w,
           fc_w_out, ssd_w_in, ssd_conv_w, ssd_conv_b, ssd_dt_bias, ssd_a_log, ssd_d,
           ssd_norm_g, ssd_w_out, final_norm_g):
    d = x_prompt.shape[-1]
    dec_batch = x_sample.shape[0]
    no, _, heads = ssd_a_log.shape
    d_inner = ssd_w_out.shape[1]
    conv_dim = ssd_conv_w.shape[-1]
    hp = d_inner // heads
    groups = heads // HEADS_PER_GROUP
    assert dec_batch + 1 <= COND_ROWS and heads % HEADS_PER_GROUP == 0

    cond = jnp.zeros((COND_ROWS, d), F32).at[:dec_batch].set(c).at[dec_batch].set(c_ctx)
    mod = _modulation(cond, w_mod, b_mod)

    zx = d_inner + conv_dim
    ssd_w = jnp.swapaxes(ssd_w_in[:, :, :zx], 1, 2).astype(BF16)
    ssd_conv_wb = jnp.swapaxes(jnp.concatenate([ssd_conv_w, ssd_conv_b[:, None]], axis=1), 1, 2)
    perm = jnp.arange(2 * heads).reshape(2, groups, HEADS_PER_GROUP).transpose(1, 0, 2).reshape(-1)
    ssd_wdt_t = jnp.swapaxes(ssd_w_in[:, :, zx:][:, :, perm], 1, 2).astype(BF16)
    ssd_dtb = ssd_dt_bias.reshape(no, 2 * heads)[:, perm][:, :, None]
    ssd_alog = ssd_a_log.reshape(no, 2 * heads)[:, perm][:, :, None]
    ssd_dcol = jnp.repeat(ssd_d, hp, axis=1)[:, :, None]
    weights = (norm_g, fc_w_in.astype(BF16), fc_conv_w, fc_w_out.astype(BF16), ssd_w, ssd_wdt_t,
               ssd_dtb, ssd_alog, ssd_conv_wb, ssd_dcol, ssd_norm_g,
               ssd_w_out.astype(BF16), final_norm_g)

    y_prompt, states = _trunk(x_prompt, mod, lambda s: dec_batch, True, x_prompt.shape[1], None,
                              True, weights)
    y_sample, _ = _trunk(x_sample, mod, lambda s: s, False, GRID_W, state_ssm, False, weights)
    return y_prompt, y_sample, states
```

```python
import functools
import math

import jax
import jax.numpy as jnp
from jax import lax
from jax.experimental import pallas as pl
from jax.experimental.pallas import tpu as pltpu

F32 = jnp.float32
BF16 = jnp.bfloat16

EPS = 1e-6
GRID_W = 64
FOURIER_GROUPS = 4
DFT_INNER = 128
CHUNK = 128
HEADS_PER_GROUP = 4
LANES = 128
TOKEN_TILE = 512
SSD_IN_TOKENS = 2048
SSD_IN_ROWS = 1024
SSD_IN_SUB_ROWS = 256
SCAN_CELL_CHUNKS = 8
COND_ROWS = 16
VMEM_LIMIT = 56 * 1024 * 1024


def _params(n_grid):
    return pltpu.CompilerParams(
        dimension_semantics=("arbitrary",) * n_grid, vmem_limit_bytes=VMEM_LIMIT)


def _dot(a, b):
    return jnp.dot(a, b, preferred_element_type=F32)


def _dot_nt(a, b):
    return lax.dot_general(a, b, (((1,), (1,)), ((), ())), preferred_element_type=F32)


def _sigmoid(x):
    return 1.0 / (1.0 + jnp.exp(-x))


def _silu(x):
    return x * _sigmoid(x)


def _rms(x):
    return x * lax.rsqrt(jnp.mean(x * x, axis=-1, keepdims=True) + EPS)


def _modnorm(x, g_ref, mod_ref):
    d = x.shape[-1]
    shift = mod_ref[:, 0:d]
    scale = mod_ref[:, d:2 * d]
    return _rms(x) * g_ref[...] * (1.0 + scale) + shift


def _residual(x, out, mod_ref, fg_ref):
    d = x.shape[-1]
    xn = x + mod_ref[:, 2 * d:3 * d] * out
    if fg_ref is not None:
        xn = _rms(xn) * fg_ref[...]
    return xn


def _mod_kernel(c_ref, w_ref, b_ref, o_ref):
    act = _silu(c_ref[...]).astype(BF16)
    o_ref[...] = _dot(act, w_ref[...].astype(BF16)) + b_ref[...]


def _modulation(cond, w_mod, b_mod):
    depth, d, d3 = w_mod.shape
    nb = d3 // d
    return pl.pallas_call(
        _mod_kernel,
        grid=(depth, nb),
        in_specs=[
            pl.BlockSpec((COND_ROWS, d), lambda l, j: (0, 0)),
            pl.BlockSpec((None, d, d), lambda l, j: (l, 0, j)),
            pl.BlockSpec((None, 1, d), lambda l, j: (l, 0, j)),
        ],
        out_specs=pl.BlockSpec((None, COND_ROWS, d), lambda l, j: (l, 0, j)),
        out_shape=jax.ShapeDtypeStruct((depth, COND_ROWS, d3), F32),
        compiler_params=_params(2),
        name="modulation",
    )(cond, w_mod, b_mod.reshape(depth, 1, d3))


def _fc_in_kernel(x_ref, g_ref, mod_ref, w_ref, cw_ref, dft_ref,
                  xc_ref, xs_ref, ga_ref, yb_ref, *, row_len, col_block):
    tm, d = x_ref.shape
    gd = d // FOURIER_GROUPS
    hm = _modnorm(x_ref[...], g_ref, mod_ref).astype(BF16)
    ua = _dot(hm, w_ref[:, 0:d]).astype(BF16)
    for g in range(FOURIER_GROUPS):
        t = _dot(ua[:, g * gd:(g + 1) * gd], dft_ref[...])
        xc_ref[:, g * gd:(g + 1) * gd] = t[:, :gd].astype(BF16)
        xs_ref[:, g * gd:(g + 1) * gd] = t[:, gd:].astype(BF16)
    pos = lax.rem(lax.broadcasted_iota(jnp.int32, (tm, 1), 0), row_len)
    first = pos == 0
    last = pos == row_len - 1
    for j in range(d // col_block):
        lo = j * col_block
        za = _dot(hm, w_ref[:, d + lo:d + lo + col_block])
        ga_ref[:, lo:lo + col_block] = _silu(za).astype(BF16)
        bb = _dot(hm, w_ref[:, 2 * d + lo:2 * d + lo + col_block])
        cc = _dot(hm, w_ref[:, 3 * d + lo:3 * d + lo + col_block])
        vv = _dot(hm, w_ref[:, 4 * d + lo:4 * d + lo + col_block])
        zb = _dot(hm, w_ref[:, 5 * d + lo:5 * d + lo + col_block])
        u = cc * vv
        up = jnp.where(first, 0.0, pltpu.roll(u, 1, 0))
        un = jnp.where(last, 0.0, pltpu.roll(u, tm - 1, 0))
        cw = cw_ref[:, lo:lo + col_block]
        y = up * cw[0:1] + u * cw[1:2] + un * cw[2:3]
        yb_ref[:, lo:lo + col_block] = (bb * y * _silu(zb)).astype(BF16)


def _fc_out_kernel(dc_ref, ds_ref, xc_ref, xs_ref, ga_ref, yb_ref, w_ref, x_ref, mod_ref,
                   *rest, final):
    fg_ref = rest[0] if final else None
    o_ref = rest[-1]
    d = x_ref.shape[1]
    ya = _dot(dc_ref[...], xc_ref[...]) + _dot(ds_ref[...], xs_ref[...])
    h1 = (ya * ga_ref[...].astype(F32)).astype(BF16)
    out = _dot(h1, w_ref[0:d, :]) + _dot(yb_ref[...], w_ref[d:2 * d, :])
    o_ref[...] = _residual(x_ref[...], out, mod_ref, fg_ref)


def _dft_tables(n, dtype=BF16):
    nb = DFT_INNER if n % DFT_INNER == 0 else 1
    k = jnp.arange(n, dtype=jnp.int32)[:, None]

    def base(cols):
        ang = ((k * cols[None, :]) % n).astype(F32) * (2.0 * math.pi / n)
        return jnp.cos(ang), jnp.sin(ang)

    ca, sa = base(jnp.arange(n // nb, dtype=jnp.int32) * nb)
    cb, sb = base(jnp.arange(nb, dtype=jnp.int32))
    ca, sa, cb, sb = ca[:, :, None], sa[:, :, None], cb[:, None, :], sb[:, None, :]
    s = 1.0 / math.sqrt(n)
    cos = ((ca * cb - sa * sb) * s).reshape(n, n)
    sin = ((sa * cb + ca * sb) * s).reshape(n, n)
    return cos.astype(dtype), sin.astype(dtype)


def _fc_layer(x2d, seq, mod3, mod_row, norm_g, w_in, conv_w, w_out, li, row_len, final_g):
    t, d = x2d.shape
    tm = min(TOKEN_TILE, t)
    tiles_per_seq = max(seq.length // tm, 1)
    gd = d // FOURIER_GROUPS
    cc, sc = _dft_tables(gd)
    dft_ch = jnp.concatenate([cc, sc], axis=1)
    row_of_tile = lambda i: mod_row(i // tiles_per_seq)
    const = lambda *_: (0, 0)
    tok = pl.BlockSpec((tm, d), lambda i: (i, 0))
    xc, xs, ga, yb = pl.pallas_call(
        functools.partial(_fc_in_kernel, row_len=row_len, col_block=min(256, d)),
        grid=(t // tm,),
        in_specs=[
            tok,
            pl.BlockSpec((1, d), const),
            pl.BlockSpec((None, 1, 3 * d), lambda i: (row_of_tile(i), 0, 0)),
            pl.BlockSpec((None, d, 6 * d), lambda i: (li, 0, 0), pipeline_mode=pl.Buffered(1)),
            pl.BlockSpec((3, d), const),
            pl.BlockSpec((gd, 2 * gd), const),
        ],
        out_specs=[tok, tok, tok, tok],
        out_shape=[jax.ShapeDtypeStruct((t, d), BF16)] * 4,
        compiler_params=_params(1),
        name="fc_in",
    )(x2d, norm_g[None], mod3, w_in, conv_w, dft_ch)

    ln = seq.length
    tr = min(TOKEN_TILE, ln)
    rt = ln // tr
    cl, sl = _dft_tables(ln)
    final = final_g is not None
    row = pl.BlockSpec((tr, d), lambda s, r: (s * rt + r, 0))
    whole = pl.BlockSpec((ln, d), lambda s, r: (s, 0))
    in_specs = [
        pl.BlockSpec((tr, ln), lambda s, r: (r, 0)),
        pl.BlockSpec((tr, ln), lambda s, r: (r, 0)),
        whole, whole, row, row,
        pl.BlockSpec((None, 2 * d, d), lambda s, r: (li, 0, 0), pipeline_mode=pl.Buffered(1)),
        row,
        pl.BlockSpec((None, 1, 3 * d), lambda s, r: (mod_row(s), 0, 0)),
    ]
    args = [cl, -sl, xc, xs, ga, yb, w_out, x2d, mod3]
    if final:
        in_specs.append(pl.BlockSpec((1, d), lambda s, r: (0, 0)))
        args.append(final_g[None])
    return pl.pallas_call(
        functools.partial(_fc_out_kernel, final=final),
        grid=(seq.count, rt),
        in_specs=in_specs,
        out_specs=row,
        out_shape=jax.ShapeDtypeStruct((t, d), F32),
        compiler_params=_params(2),
        name="fc_out",
    )(*args)


TAB_DT, TAB_CUM2, TAB_ROW2, TAB_COEF, TAB_E, TAB_CD = range(6)
LOG2_E = math.log2(math.e)
LOG2_FLOOR = -1e30


def _ssd_in_kernel(x_ref, g_ref, mod_ref, wt_ref, wdt_ref, dtb_ref, alog_ref, cw_ref,
                   proj_ref, tab_ref, hm_s, *, seq_len, gate_blocks, sub_rows):
    j = pl.program_id(1)
    nt = x_ref.shape[0]
    rows = wt_ref.shape[0]
    nc = nt // CHUNK

    def emit(r0, y):
        yb = y.astype(BF16)
        for k in range(nc):
            proj_ref[k, pl.ds(r0, sub_rows), :] = yb[:, k * CHUNK:(k + 1) * CHUNK]

    @pl.when(j == 0)
    def _():
        hm = _modnorm(x_ref[...], g_ref, mod_ref).astype(BF16)
        hm_s[...] = hm
        _scan_tables(hm, wdt_ref, dtb_ref, alog_ref, tab_ref)

    @pl.when(j < gate_blocks)
    def _():
        for r0 in range(0, rows, sub_rows):
            emit(r0, _silu(_dot_nt(wt_ref[r0:r0 + sub_rows, :], hm_s[...])))

    @pl.when(j >= gate_blocks)
    def _():
        lane0 = lax.broadcasted_iota(jnp.int32, (sub_rows, LANES), 1)

        @pl.loop(0, rows // sub_rows)
        def _(i):
            r0 = pl.multiple_of(i * sub_rows, sub_rows)
            p = _dot_nt(wt_ref[pl.ds(r0, sub_rows), :], hm_s[...])
            left = pltpu.roll(p, 1, 1)
            right = pltpu.roll(p, nt - 1, 1)
            lcols = [left[:, q:q + LANES] for q in range(0, nt, LANES)]
            rcols = [right[:, q:q + LANES] for q in range(0, nt, LANES)]
            for q in range(0, nt, seq_len):
                lcols[q // LANES] = jnp.where(lane0 == 0, 0.0, lcols[q // LANES])
                e = (q + seq_len) // LANES - 1
                rcols[e] = jnp.where(lane0 == LANES - 1, 0.0, rcols[e])
            left = jnp.concatenate(lcols, axis=1)
            right = jnp.concatenate(rcols, axis=1)
            cw = cw_ref[pl.ds(r0, sub_rows), :]
            y = left * cw[:, 0:1] + p * cw[:, 1:2] + right * cw[:, 2:3] + cw[:, 3:4]
            emit(r0, _silu(y))


def _scan_tables(hm, wdt_ref, dtb_ref, alog_ref, tab_ref):
    v = _dot_nt(wdt_ref[...], hm) + dtb_ref[...]
    dt_all = jnp.maximum(v, 0.0) + jnp.log(1.0 + jnp.exp(-jnp.abs(v)))
    heads2, nt = dt_all.shape
    nc = nt // CHUNK
    la_all = dt_all * -jnp.exp(alog_ref[...])
    dt = jnp.concatenate([dt_all[:, k * CHUNK:(k + 1) * CHUNK] for k in range(nc)], axis=0)
    la = jnp.concatenate([la_all[:, k * CHUNK:(k + 1) * CHUNK] for k in range(nc)], axis=0)
    p0 = la.astype(BF16)
    r1 = la - p0.astype(F32)
    p1 = r1.astype(BF16)
    p2 = (r1 - p1.astype(F32)).astype(BF16)
    ji = lax.broadcasted_iota(jnp.int32, (CHUNK, CHUNK), 0)
    li = lax.broadcasted_iota(jnp.int32, (CHUNK, CHUNK), 1)
    upto = jnp.where(ji <= li, 1.0, 0.0).astype(BF16)
    from_ = jnp.where(ji >= li, 1.0, 0.0).astype(BF16)
    pre = _dot(p0, upto) + _dot(p1, upto) + _dot(p2, upto)
    suf = _dot(p0, from_) + _dot(p1, from_) + _dot(p2, from_)
    row = lax.broadcasted_iota(jnp.int32, la.shape, 0)
    is_fwd = (row & (2 * HEADS_PER_GROUP - 1)) < HEADS_PER_GROUP
    cum = jnp.where(is_fwd, pre, suf)
    tot = jnp.where(is_fwd, jnp.broadcast_to(cum[:, CHUNK - 1:CHUNK], cum.shape),
                    jnp.broadcast_to(cum[:, 0:1], cum.shape))
    coef = dt * jnp.exp(tot - cum)
    e = jnp.exp(cum)
    cd = jnp.exp(tot)
    cum2 = cum * LOG2_E
    row2 = cum2 - jnp.maximum(jnp.log2(dt), LOG2_FLOOR)
    for k in range(nc):
        rows = slice(k * heads2, (k + 1) * heads2)
        tab_ref[k, TAB_DT] = dt[rows]
        tab_ref[k, TAB_CUM2] = cum2[rows]
        tab_ref[k, TAB_ROW2] = row2[rows]
        tab_ref[k, TAB_COEF] = coef[rows]
        tab_ref[k, TAB_E] = e[rows]
        tab_ref[k, TAB_CD] = cd[rows]


def _ssd_scan_kernel(sz_ref, x_ref, b_ref, c_ref, tab_ref, dcol_ref, gcol_ref,
                     *rest, seq_chunks, has_h0, has_prev, emit_state, unroll):
    rest = list(rest)
    h0_ref = rest.pop(0) if has_h0 else None
    prev_ref = rest.pop(0) if has_prev else None
    y_ref = rest.pop(0)
    st_ref = rest.pop(0) if emit_state else None
    yp_s, stf_s, stb_s, hf_s, hb_s, seg_s, cb_s, bm_s = rest

    nc, gw, _ = x_ref.shape
    n = b_ref.shape[1]
    hp = gw // HEADS_PER_GROUP
    nh = HEADS_PER_GROUP
    r2 = 2 * nh
    assert not has_h0 or nc == seq_chunks

    li = lax.broadcasted_iota(jnp.int32, (CHUNK, CHUNK), 0)
    si = lax.broadcasted_iota(jnp.int32, (CHUNK, CHUNK), 1)

    def expand(rows4):
        return jnp.concatenate(
            [jnp.broadcast_to(rows4[h:h + 1, :], (hp, LANES)) for h in range(nh)], axis=0)

    blk = (lax.broadcasted_iota(jnp.int32, (r2, r2 * CHUNK), 1) // CHUNK
           == lax.broadcasted_iota(jnp.int32, (r2, r2 * CHUNK), 0))
    sel_k = jnp.concatenate([jnp.where(blk, 1.0, 0.0)] * 3, axis=0)
    ones_k = jnp.ones((3 * r2, CHUNK), F32)

    def split3(v):
        p0 = v.astype(BF16).astype(F32)
        p1 = (v - p0).astype(BF16).astype(F32)
        return p0, p1, v - p0 - p1

    def seg_chunk(c, carry):
        lhs = jnp.concatenate([-p for p in split3(tab_ref[c, TAB_ROW2])] + [ones_k], axis=0)
        rhs = jnp.concatenate(
            [sel_k] + [jnp.where(blk, jnp.concatenate([p] * r2, axis=1), 0.0)
                       for p in split3(tab_ref[c, TAB_CUM2])], axis=0)
        seg_s[c] = lax.dot_general(lhs.astype(BF16), rhs.astype(BF16), (((0,), (0,)), ((), ())),
                                   preferred_element_type=F32)
        bm = b_ref[c].T
        bm_s[c] = bm
        cb_s[c] = _dot(bm, c_ref[c])
        return carry

    lax.fori_loop(0, nc, seg_chunk, 0, unroll=unroll)

    def local_chunk(c, carry):
        xst_b = x_ref[c]
        xst = xst_b.astype(F32)
        cb = cb_s[c]
        parts = []
        for h in range(nh):
            hb = nh + h
            arg = jnp.where(li <= si, seg_s[c, :, h * CHUNK:(h + 1) * CHUNK],
                            seg_s[c, :, hb * CHUNK:(hb + 1) * CHUNK])
            w = (cb * jnp.exp2(arg)).astype(BF16)
            parts.append(_dot(xst_b[h * hp:(h + 1) * hp, :], w))
        cb_diag = jnp.sum(jnp.where(si == li, cb, 0.0), axis=0, keepdims=True)
        skip = dcol_ref[...] + expand(tab_ref[c, TAB_DT][nh:r2] * cb_diag)
        yp_s[c] = jnp.concatenate(parts, axis=0) + skip * xst

        coef8 = tab_ref[c, TAB_COEF]
        xdw = jnp.concatenate([xst * expand(coef8[0:nh]), xst * expand(coef8[nh:r2])], axis=0)
        st = _dot(xdw.astype(BF16), bm_s[c])
        stf_s[c] = st[0:gw]
        stb_s[c] = st[gw:2 * gw]
        return carry

    lax.fori_loop(0, nc, local_chunk, 0, unroll=unroll)

    if has_h0:
        h0f = h0_ref[0].reshape(gw, n)
        h0b = h0_ref[1].reshape(gw, n)
    else:
        h0f = jnp.zeros((gw, n), F32)
        h0b = h0f
    if has_prev:
        st_ref[:, 0:prev_ref.shape[1]] = prev_ref[...]
    slot = st_ref.shape[1] - 1 if emit_state else None

    def fwd_state(c, h):
        if nc > seq_chunks:
            h = jnp.where(c % seq_chunks == 0, 0.0, h)
        hf_s[c] = h.astype(BF16)
        h = h * expand(tab_ref[c, TAB_CD][0:nh]) + stf_s[c]
        if emit_state:
            @pl.when(c % seq_chunks == seq_chunks - 1)
            def _():
                st_ref[c // seq_chunks, slot, 0] = h.reshape(nh, hp, n)
        return h

    def bwd_state(c, h):
        if nc > seq_chunks:
            h = jnp.where(c % seq_chunks == seq_chunks - 1, 0.0, h)
        hb_s[c] = h.astype(BF16)
        h = h * expand(tab_ref[c, TAB_CD][nh:r2]) + stb_s[c]
        if emit_state:
            @pl.when(c % seq_chunks == 0)
            def _():
                st_ref[c // seq_chunks, slot, 1] = h.reshape(nh, hp, n)
        return h

    lax.fori_loop(0, nc, fwd_state, h0f)
    lax.fori_loop(0, nc, lambda i, h: bwd_state(nc - 1 - i, h), h0b)

    def output_chunk(c, carry):
        hin = jnp.concatenate([hf_s[c], hb_s[c]], axis=0)
        yo = _dot(hin, c_ref[c])
        e8 = tab_ref[c, TAB_E]
        y_t = yp_s[c] + yo[0:gw] * expand(e8[0:nh]) + yo[gw:2 * gw] * expand(e8[nh:r2])
        y = y_t * sz_ref[c].astype(F32)
        inv = lax.rsqrt(jnp.mean(y * y, axis=0, keepdims=True) + EPS)
        y_ref[c] = (y * inv * gcol_ref[...]).astype(BF16)
        return carry

    lax.fori_loop(0, nc, output_chunk, 0, unroll=unroll)


def _ssd_out_kernel(y_ref, w_ref, x_ref, mod_ref, *rest, final):
    fg_ref = rest[0] if final else None
    o_ref = rest[-1]
    y = jnp.concatenate([y_ref[k].T for k in range(y_ref.shape[0])], axis=0)
    out = _dot(y, w_ref[...])
    o_ref[...] = _residual(x_ref[...], out, mod_ref, fg_ref)


def _ssd_layer(x2d, seq, mod3, mod_row, norm_g, w_zxbc, wdt_t, dt_bias_col, alog_col,
               conv_wb, d_col, gn_g, w_out, h0, prev_state, layer_idx, emit_state, final_g):
    t, d = x2d.shape
    d_inner = w_out.shape[1]
    heads2 = wdt_t.shape[1]
    groups = heads2 // (2 * HEADS_PER_GROUP)
    gw = d_inner // groups
    n = (w_zxbc.shape[1] - 2 * d_inner) // (2 * groups)
    hp = gw // HEADS_PER_GROUP
    ln = seq.length
    nc = ln // CHUNK
    tm = min(TOKEN_TILE, t)
    tiles_per_seq = max(ln // tm, 1)
    row_of_tile = lambda i: mod_row(i // tiles_per_seq)
    const = lambda *_: (0, 0)
    tok = pl.BlockSpec((tm, d), lambda i: (i, 0))
    wrows = w_zxbc.shape[1]

    nt = ln * max(1, min(SSD_IN_TOKENS, t) // ln) if seq.shared_cond else ln
    seqs_per_block = nt // ln
    rows_blk = min(SSD_IN_ROWS, d_inner)
    assert t % nt == 0 and d_inner % rows_blk == 0 and wrows % rows_blk == 0
    n_tab = TAB_CD + 1
    gate_blocks = d_inner // rows_blk
    proj, tab = pl.pallas_call(
        functools.partial(_ssd_in_kernel, seq_len=ln, gate_blocks=gate_blocks,
                          sub_rows=min(SSD_IN_SUB_ROWS, rows_blk)),
        grid=(t // nt, wrows // rows_blk),
        in_specs=[
            pl.BlockSpec((nt, d), lambda i, j: (i, 0), pipeline_mode=pl.Buffered(1)),
            pl.BlockSpec((1, d), const),
            pl.BlockSpec((None, 1, 3 * d), lambda i, j: (mod_row(i * seqs_per_block), 0, 0)),
            pl.BlockSpec((None, rows_blk, d), lambda i, j: (layer_idx, j, 0)),
            pl.BlockSpec((None, heads2, d), lambda i, j: (layer_idx, 0, 0)),
            pl.BlockSpec((heads2, 1), const),
            pl.BlockSpec((heads2, 1), const),
            pl.BlockSpec((rows_blk, 4), lambda i, j: (jnp.maximum(j - gate_blocks, 0), 0)),
        ],
        out_specs=[
            pl.BlockSpec((nt // CHUNK, rows_blk, CHUNK), lambda i, j: (i, j, 0)),
            pl.BlockSpec((nt // CHUNK, n_tab, heads2, CHUNK), lambda i, j: (i, 0, 0, 0)),
        ],
        out_shape=[
            jax.ShapeDtypeStruct((t // CHUNK, wrows, CHUNK), BF16),
            jax.ShapeDtypeStruct((t // CHUNK, n_tab, heads2, CHUNK), F32),
        ],
        scratch_shapes=[pltpu.VMEM((nt, d), BF16)],
        compiler_params=_params(2),
        name="ssd_in",
    )(x2d, norm_g[None], mod3, w_zxbc, wdt_t, dt_bias_col, alog_col, conv_wb)

    r2 = 2 * HEADS_PER_GROUP
    xg0 = d_inner // gw
    bg0 = 2 * d_inner // n
    cg0 = bg0 + groups
    has_h0 = h0 is not None
    has_prev = emit_state and prev_state is not None
    spc = 1 if has_h0 else max(1, min(seq.count, SCAN_CELL_CHUNKS // nc))
    assert seq.count % spc == 0
    cc = spc * nc
    nh = HEADS_PER_GROUP
    in_specs = [
        pl.BlockSpec((cc, gw, CHUNK), lambda s, g: (s, g, 0)),
        pl.BlockSpec((cc, gw, CHUNK), lambda s, g: (s, xg0 + g, 0)),
        pl.BlockSpec((cc, n, CHUNK), lambda s, g: (s, bg0 + g, 0)),
        pl.BlockSpec((cc, n, CHUNK), lambda s, g: (s, cg0 + g, 0)),
        pl.BlockSpec((cc, n_tab, r2, CHUNK), lambda s, g: (s, 0, g, 0)),
        pl.BlockSpec((gw, LANES), lambda s, g: (g, 0)),
        pl.BlockSpec((gw, LANES), lambda s, g: (g, 0)),
    ]
    args = [proj, proj, proj, proj, tab, jnp.broadcast_to(d_col, (d_inner, LANES)),
            jnp.broadcast_to(gn_g[:, None], (d_inner, LANES))]
    if has_h0:
        in_specs.append(pl.BlockSpec((None, None, 2, nh, hp, n),
                                     lambda s, g: (s, layer_idx, 0, g, 0, 0)))
        args.append(h0)
    if has_prev:
        in_specs.append(pl.BlockSpec((spc, prev_state.shape[1], 2, nh, hp, n),
                                     lambda s, g: (s, 0, 0, g, 0, 0)))
        args.append(prev_state)
    out_specs = [pl.BlockSpec((cc, gw, CHUNK), lambda s, g: (s, g, 0))]
    out_shape = [jax.ShapeDtypeStruct((t // CHUNK, d_inner, CHUNK), BF16)]
    if emit_state:
        slots = (prev_state.shape[1] if has_prev else 0) + 1
        out_specs.append(pl.BlockSpec((spc, slots, 2, nh, hp, n), lambda s, g: (s, 0, 0, g, 0, 0)))
        out_shape.append(jax.ShapeDtypeStruct((seq.count, slots, 2, groups * nh, hp, n), F32))
    res = pl.pallas_call(
        functools.partial(_ssd_scan_kernel, seq_chunks=nc, has_h0=has_h0, has_prev=has_prev,
                          emit_state=emit_state, unroll=min(16, cc)),
        grid=(seq.count // spc, groups),
        in_specs=in_specs,
        out_specs=out_specs,
        out_shape=out_shape,
        scratch_shapes=[
            pltpu.VMEM((cc, gw, CHUNK), F32),
            pltpu.VMEM((cc, gw, n), F32),
            pltpu.VMEM((cc, gw, n), F32),
            pltpu.VMEM((cc, gw, n), BF16),
            pltpu.VMEM((cc, gw, n), BF16),
            pltpu.VMEM((cc, CHUNK, r2 * CHUNK), F32),
            pltpu.VMEM((cc, CHUNK, CHUNK), F32),
            pltpu.VMEM((cc, CHUNK, n), BF16),
        ],
        compiler_params=_params(2),
        name="ssd_scan",
    )(*args)
    y, state = (res[0], res[1]) if emit_state else (res[0], None)

    final = final_g is not None
    in_specs = [
        pl.BlockSpec((tm // CHUNK, d_inner, CHUNK), lambda i: (i, 0, 0)),
        pl.BlockSpec((None, d_inner, d), lambda i: (layer_idx, 0, 0), pipeline_mode=pl.Buffered(1)),
        tok,
        pl.BlockSpec((None, 1, 3 * d), lambda i: (row_of_tile(i), 0, 0)),
    ]
    args = [y, w_out, x2d, mod3]
    if final:
        in_specs.append(pl.BlockSpec((1, d), const))
        args.append(final_g[None])
    x_new = pl.pallas_call(
        functools.partial(_ssd_out_kernel, final=final),
        grid=(t // tm,),
        in_specs=in_specs,
        out_specs=tok,
        out_shape=jax.ShapeDtypeStruct((t, d), F32),
        compiler_params=_params(1),
        name="ssd_out",
    )(*args)
    return x_new, state


class _Seqs:
    def __init__(self, count, length, shared_cond):
        self.count = count
        self.length = length
        self.shared_cond = shared_cond


def _trunk(x, mod, mod_row, shared_cond, row_len, h0, emit_state, weights):
    (norm_g, fc_w_in, fc_conv_w, fc_w_out, ssd_w, ssd_wdt_t, ssd_dtb, ssd_alog, ssd_conv_wb,
     ssd_dcol, ssd_norm_g, ssd_w_out, final_norm_g) = weights
    bsz, ln, d = x.shape
    seq = _Seqs(bsz, ln, shared_cond)
    x2d = x.reshape(bsz * ln, d)
    depth = norm_g.shape[0]
    states = None
    for layer in range(depth):
        i = layer // 2
        mod3 = mod[layer].reshape(COND_ROWS, 1, 3 * d)
        final_g = final_norm_g if layer == depth - 1 else None
        if layer % 2 == 0:
            x2d = _fc_layer(x2d, seq, mod3, mod_row, norm_g[layer], fc_w_in, fc_conv_w[i],
                            fc_w_out, i, row_len, final_g)
        else:
            x2d, st = _ssd_layer(x2d, seq, mod3, mod_row, norm_g[layer], ssd_w, ssd_wdt_t,
                                 ssd_dtb[i], ssd_alog[i], ssd_conv_wb[i], ssd_dcol[i],
                                 ssd_norm_g[i], ssd_w_out, h0, states, i, emit_state, final_g)
            states = st
    return x2d.reshape(bsz, ln, d), states


def kernel(x_prompt, x_sample, state_ssm, c, c_ctx, w_mod, b_mod, norm_g, fc_w_in, fc_conv_w,
           fc_w_out, ssd_w_in, ssd_conv_w, ssd_conv_b, ssd_dt_bias, ssd_a_log, ssd_d,
           ssd_norm_g, ssd_w_out, final_norm_g):
    d = x_prompt.shape[-1]
    dec_batch = x_sample.shape[0]
    no, _, heads = ssd_a_log.shape
    d_inner = ssd_w_out.shape[1]
    conv_dim = ssd_conv_w.shape[-1]
    hp = d_inner // heads
    groups = heads // HEADS_PER_GROUP
    assert dec_batch + 1 <= COND_ROWS and heads % HEADS_PER_GROUP == 0

    cond = jnp.zeros((COND_ROWS, d), F32).at[:dec_batch].set(c).at[dec_batch].set(c_ctx)
    mod = _modulation(cond, w_mod, b_mod)

    zx = d_inner + conv_dim
    ssd_w = jnp.swapaxes(ssd_w_in[:, :, :zx], 1, 2).astype(BF16)
    ssd_conv_wb = jnp.swapaxes(jnp.concatenate([ssd_conv_w, ssd_conv_b[:, None]], axis=1), 1, 2)
    perm = jnp.arange(2 * heads).reshape(2, groups, HEADS_PER_GROUP).transpose(1, 0, 2).reshape(-1)
    ssd_wdt_t = jnp.swapaxes(ssd_w_in[:, :, zx:][:, :, perm], 1, 2).astype(BF16)
    ssd_dtb = ssd_dt_bias.reshape(no, 2 * heads)[:, perm][:, :, None]
    ssd_alog = ssd_a_log.reshape(no, 2 * heads)[:, perm][:, :, None]
    ssd_dcol = jnp.repeat(ssd_d, hp, axis=1)[:, :, None]
    weights = (norm_g, fc_w_in.astype(BF16), fc_conv_w, fc_w_out.astype(BF16), ssd_w, ssd_wdt_t,
               ssd_dtb, ssd_alog, ssd_conv_wb, ssd_dcol, ssd_norm_g,
               ssd_w_out.astype(BF16), final_norm_g)

    y_prompt, states = _trunk(x_prompt, mod, lambda s: dec_batch, True, x_prompt.shape[1], None,
                              True, weights)
    y_sample, _ = _trunk(x_sample, mod, lambda s: s, False, GRID_W, state_ssm, False, weights)
    return y_prompt, y_sample, states
```

```python
import functools
import math

import jax
import jax.numpy as jnp
from jax import lax
from jax.experimental import pallas as pl
from jax.experimental.pallas import tpu as pltpu

F32 = jnp.float32
BF16 = jnp.bfloat16

EPS = 1e-6
GRID_W = 64
FOURIER_GROUPS = 4
DFT_INNER = 128
CHUNK = 128
HEADS_PER_GROUP = 4
LANES = 128
TOKEN_TILE = 512
SSD_IN_TOKENS = 2048
SSD_IN_ROWS = 1024
SSD_IN_SUB_ROWS = 256
SCAN_CELL_CHUNKS = 8
COND_ROWS = 16
VMEM_LIMIT = 56 * 1024 * 1024


def _params(n_grid):
    return pltpu.CompilerParams(
        dimension_semantics=("arbitrary",) * n_grid, vmem_limit_bytes=VMEM_LIMIT)


def _dot(a, b):
    return jnp.dot(a, b, preferred_element_type=F32)


def _dot_nt(a, b):
    return lax.dot_general(a, b, (((1,), (1,)), ((), ())), preferred_element_type=F32)


def _sigmoid(x):
    return 1.0 / (1.0 + jnp.exp(-x))


def _silu(x):
    return x * _sigmoid(x)


def _rms(x):
    return x * lax.rsqrt(jnp.mean(x * x, axis=-1, keepdims=True) + EPS)


def _modnorm(x, g_ref, mod_ref):
    d = x.shape[-1]
    shift = mod_ref[:, 0:d]
    scale = mod_ref[:, d:2 * d]
    return _rms(x) * g_ref[...] * (1.0 + scale) + shift


def _residual(x, out, mod_ref, fg_ref):
    d = x.shape[-1]
    xn = x + mod_ref[:, 2 * d:3 * d] * out
    if fg_ref is not None:
        xn = _rms(xn) * fg_ref[...]
    return xn


def _mod_kernel(c_ref, w_ref, b_ref, o_ref):
    act = _silu(c_ref[...]).astype(BF16)
    o_ref[...] = _dot(act, w_ref[...].astype(BF16)) + b_ref[...]


def _modulation(cond, w_mod, b_mod):
    depth, d, d3 = w_mod.shape
    nb = d3 // d
    return pl.pallas_call(
        _mod_kernel,
        grid=(depth, nb),
        in_specs=[
            pl.BlockSpec((COND_ROWS, d), lambda l, j: (0, 0)),
            pl.BlockSpec((None, d, d), lambda l, j: (l, 0, j)),
            pl.BlockSpec((None, 1, d), lambda l, j: (l, 0, j)),
        ],
        out_specs=pl.BlockSpec((None, COND_ROWS, d), lambda l, j: (l, 0, j)),
        out_shape=jax.ShapeDtypeStruct((depth, COND_ROWS, d3), F32),
        compiler_params=_params(2),
        name="modulation",
    )(cond, w_mod, b_mod.reshape(depth, 1, d3))


def _fc_in_kernel(x_ref, g_ref, mod_ref, w_ref, cw_ref, dft_ref,
                  xc_ref, xs_ref, ga_ref, yb_ref, *, row_len, col_block):
    tm, d = x_ref.shape
    gd = d // FOURIER_GROUPS
    hm = _modnorm(x_ref[...], g_ref, mod_ref).astype(BF16)
    ua = _dot(hm, w_ref[:, 0:d]).astype(BF16)
    for g in range(FOURIER_GROUPS):
        t = _dot(ua[:, g * gd:(g + 1) * gd], dft_ref[...])
        xc_ref[:, g * gd:(g + 1) * gd] = t[:, :gd].astype(BF16)
        xs_ref[:, g * gd:(g + 1) * gd] = t[:, gd:].astype(BF16)
    pos = lax.rem(lax.broadcasted_iota(jnp.int32, (tm, 1), 0), row_len)
    first = pos == 0
    last = pos == row_len - 1
    for j in range(d // col_block):
        lo = j * col_block
        za = _dot(hm, w_ref[:, d + lo:d + lo + col_block])
        ga_ref[:, lo:lo + col_block] = _silu(za).astype(BF16)
        bb = _dot(hm, w_ref[:, 2 * d + lo:2 * d + lo + col_block])
        cc = _dot(hm, w_ref[:, 3 * d + lo:3 * d + lo + col_block])
        vv = _dot(hm, w_ref[:, 4 * d + lo:4 * d + lo + col_block])
        zb = _dot(hm, w_ref[:, 5 * d + lo:5 * d + lo + col_block])
        u = cc * vv
        up = jnp.where(first, 0.0, pltpu.roll(u, 1, 0))
        un = jnp.where(last, 0.0, pltpu.roll(u, tm - 1, 0))
        cw = cw_ref[:, lo:lo + col_block]
        y = up * cw[0:1] + u * cw[1:2] + un * cw[2:3]
        yb_ref[:, lo:lo + col_block] = (bb * y * _silu(zb)).astype(BF16)


def _fc_out_kernel(dc_ref, ds_ref, rev_ref, xc_ref, xs_ref, ga_ref, yb_ref, w_ref, x_ref, mod_ref,
                   *rest, final):
    fg_ref = rest[0] if final else None
    o_ref, xcf_s, xsf_s = rest[-3:]
    tr, d = x_ref.shape
    half = rev_ref.shape[0]

    @pl.when(pl.program_id(1) == 0)
    def _():
        xcf_s[...] = (xc_ref[0:half, :].astype(F32)
                      + _dot(rev_ref[...], xc_ref[half:2 * half, :])).astype(BF16)
        xsf_s[...] = (xs_ref[0:half, :].astype(F32)
                      - _dot(rev_ref[...], xs_ref[half:2 * half, :])).astype(BF16)

    odd = lax.broadcasted_iota(jnp.int32, (tr, 1), 0) & 1
    alt = jnp.where(odd == 1, -1.0, 1.0) * (1.0 / math.sqrt(2 * half))
    ya = (_dot(dc_ref[...], xcf_s[...]) + _dot(ds_ref[...], xsf_s[...])
          + alt * xc_ref[half:half + 1, :].astype(F32))
    h1 = (ya * ga_ref[...].astype(F32)).astype(BF16)
    out = _dot(h1, w_ref[0:d, :]) + _dot(yb_ref[...], w_ref[d:2 * d, :])
    o_ref[...] = _residual(x_ref[...], out, mod_ref, fg_ref)


def _dft_tables(n, dtype=BF16):
    nb = DFT_INNER if n % DFT_INNER == 0 else 1
    k = jnp.arange(n, dtype=jnp.int32)[:, None]

    def base(cols):
        ang = ((k * cols[None, :]) % n).astype(F32) * (2.0 * math.pi / n)
        return jnp.cos(ang), jnp.sin(ang)

    ca, sa = base(jnp.arange(n // nb, dtype=jnp.int32) * nb)
    cb, sb = base(jnp.arange(nb, dtype=jnp.int32))
    ca, sa, cb, sb = ca[:, :, None], sa[:, :, None], cb[:, None, :], sb[:, None, :]
    s = 1.0 / math.sqrt(n)
    cos = ((ca * cb - sa * sb) * s).reshape(n, n)
    sin = ((sa * cb + ca * sb) * s).reshape(n, n)
    return cos.astype(dtype), sin.astype(dtype)


def _fc_layer(x2d, seq, mod3, mod_row, norm_g, w_in, conv_w, w_out, li, row_len, final_g):
    t, d = x2d.shape
    tm = min(TOKEN_TILE, t)
    tiles_per_seq = max(seq.length // tm, 1)
    gd = d // FOURIER_GROUPS
    cc, sc = _dft_tables(gd)
    dft_ch = jnp.concatenate([cc, sc], axis=1)
    row_of_tile = lambda i: mod_row(i // tiles_per_seq)
    const = lambda *_: (0, 0)
    tok = pl.BlockSpec((tm, d), lambda i: (i, 0))
    xc, xs, ga, yb = pl.pallas_call(
        functools.partial(_fc_in_kernel, row_len=row_len, col_block=min(256, d)),
        grid=(t // tm,),
        in_specs=[
            tok,
            pl.BlockSpec((1, d), const),
            pl.BlockSpec((None, 1, 3 * d), lambda i: (row_of_tile(i), 0, 0)),
            pl.BlockSpec((None, d, 6 * d), lambda i: (li, 0, 0), pipeline_mode=pl.Buffered(1)),
            pl.BlockSpec((3, d), const),
            pl.BlockSpec((gd, 2 * gd), const),
        ],
        out_specs=[tok, tok, tok, tok],
        out_shape=[jax.ShapeDtypeStruct((t, d), BF16)] * 4,
        compiler_params=_params(1),
        name="fc_in",
    )(x2d, norm_g[None], mod3, w_in, conv_w, dft_ch)

    ln = seq.length
    tr = min(TOKEN_TILE, ln)
    rt = ln // tr
    cl, sl = _dft_tables(ln)
    final = final_g is not None
    row = pl.BlockSpec((tr, d), lambda s, r: (s * rt + r, 0))
    whole = pl.BlockSpec((ln, d), lambda s, r: (s, 0))
    half = ln // 2
    assert ln % 2 == 0 and tr % 2 == 0
    ni = jnp.arange(half, dtype=jnp.int32)
    rev = jnp.where((ni[:, None] >= 1) & (ni[None, :] == half - ni[:, None]), 1.0, 0.0).astype(BF16)
    in_specs = [
        pl.BlockSpec((tr, half), lambda s, r: (r, 0)),
        pl.BlockSpec((tr, half), lambda s, r: (r, 0)),
        pl.BlockSpec((half, half), lambda s, r: (0, 0), pipeline_mode=pl.Buffered(1)),
        whole, whole, row, row,
        pl.BlockSpec((None, 2 * d, d), lambda s, r: (li, 0, 0), pipeline_mode=pl.Buffered(1)),
        row,
        pl.BlockSpec((None, 1, 3 * d), lambda s, r: (mod_row(s), 0, 0)),
    ]
    args = [cl, -sl, rev, xc, xs, ga, yb, w_out, x2d, mod3]
    if final:
        in_specs.append(pl.BlockSpec((1, d), lambda s, r: (0, 0)))
        args.append(final_g[None])
    return pl.pallas_call(
        functools.partial(_fc_out_kernel, final=final),
        grid=(seq.count, rt),
        in_specs=in_specs,
        out_specs=row,
        out_shape=jax.ShapeDtypeStruct((t, d), F32),
        scratch_shapes=[pltpu.VMEM((half, d), BF16), pltpu.VMEM((half, d), BF16)],
        compiler_params=_params(2),
        name="fc_out",
    )(*args)


TAB_DT, TAB_CUM2, TAB_ROW2, TAB_COEF, TAB_E, TAB_CD = range(6)
LOG2_E = math.log2(math.e)
LOG2_FLOOR = -1e30


def _ssd_in_kernel(x_ref, g_ref, mod_ref, wt_ref, wdt_ref, dtb_ref, alog_ref, cw_ref,
                   proj_ref, tab_ref, hm_s, *, seq_len, gate_blocks, sub_rows):
    j = pl.program_id(1)
    nt = x_ref.shape[0]
    rows = wt_ref.shape[0]
    nc = nt // CHUNK

    def emit(r0, y):
        yb = y.astype(BF16)
        for k in range(nc):
            proj_ref[k, pl.ds(r0, sub_rows), :] = yb[:, k * CHUNK:(k + 1) * CHUNK]

    @pl.when(j == 0)
    def _():
        hm = _modnorm(x_ref[...], g_ref, mod_ref).astype(BF16)
        hm_s[...] = hm
        _scan_tables(hm, wdt_ref, dtb_ref, alog_ref, tab_ref)

    @pl.when(j < gate_blocks)
    def _():
        for r0 in range(0, rows, sub_rows):
            emit(r0, _silu(_dot_nt(wt_ref[r0:r0 + sub_rows, :], hm_s[...])))

    @pl.when(j >= gate_blocks)
    def _():
        lane0 = lax.broadcasted_iota(jnp.int32, (sub_rows, LANES), 1)

        @pl.loop(0, rows // sub_rows)
        def _(i):
            r0 = pl.multiple_of(i * sub_rows, sub_rows)
            p = _dot_nt(wt_ref[pl.ds(r0, sub_rows), :], hm_s[...])
            left = pltpu.roll(p, 1, 1)
            right = pltpu.roll(p, nt - 1, 1)
            lcols = [left[:, q:q + LANES] for q in range(0, nt, LANES)]
            rcols = [right[:, q:q + LANES] for q in range(0, nt, LANES)]
            for q in range(0, nt, seq_len):
                lcols[q // LANES] = jnp.where(lane0 == 0, 0.0, lcols[q // LANES])
                e = (q + seq_len) // LANES - 1
                rcols[e] = jnp.where(lane0 == LANES - 1, 0.0, rcols[e])
            left = jnp.concatenate(lcols, axis=1)
            right = jnp.concatenate(rcols, axis=1)
            cw = cw_ref[pl.ds(r0, sub_rows), :]
            y = left * cw[:, 0:1] + p * cw[:, 1:2] + right * cw[:, 2:3] + cw[:, 3:4]
            emit(r0, _silu(y))


def _scan_tables(hm, wdt_ref, dtb_ref, alog_ref, tab_ref):
    v = _dot_nt(wdt_ref[...], hm) + dtb_ref[...]
    dt_all = jnp.maximum(v, 0.0) + jnp.log(1.0 + jnp.exp(-jnp.abs(v)))
    heads2, nt = dt_all.shape
    nc = nt // CHUNK
    la_all = dt_all * -jnp.exp(alog_ref[...])
    dt = jnp.concatenate([dt_all[:, k * CHUNK:(k + 1) * CHUNK] for k in range(nc)], axis=0)
    la = jnp.concatenate([la_all[:, k * CHUNK:(k + 1) * CHUNK] for k in range(nc)], axis=0)
    p0 = la.astype(BF16)
    r1 = la - p0.astype(F32)
    p1 = r1.astype(BF16)
    p2 = (r1 - p1.astype(F32)).astype(BF16)
    ji = lax.broadcasted_iota(jnp.int32, (CHUNK, CHUNK), 0)
    li = lax.broadcasted_iota(jnp.int32, (CHUNK, CHUNK), 1)
    upto = jnp.where(ji <= li, 1.0, 0.0).astype(BF16)
    from_ = jnp.where(ji >= li, 1.0, 0.0).astype(BF16)
    pre = _dot(p0, upto) + _dot(p1, upto) + _dot(p2, upto)
    suf = _dot(p0, from_) + _dot(p1, from_) + _dot(p2, from_)
    row = lax.broadcasted_iota(jnp.int32, la.shape, 0)
    is_fwd = (row & (2 * HEADS_PER_GROUP - 1)) < HEADS_PER_GROUP
    cum = jnp.where(is_fwd, pre, suf)
    tot = jnp.where(is_fwd, jnp.broadcast_to(cum[:, CHUNK - 1:CHUNK], cum.shape),
                    jnp.broadcast_to(cum[:, 0:1], cum.shape))
    coef = dt * jnp.exp(tot - cum)
    e = jnp.exp(cum)
    cd = jnp.exp(tot)
    cum2 = cum * LOG2_E
    row2 = cum2 - jnp.maximum(jnp.log2(dt), LOG2_FLOOR)
    for k in range(nc):
        rows = slice(k * heads2, (k + 1) * heads2)
        tab_ref[k, TAB_DT] = dt[rows]
        tab_ref[k, TAB_CUM2] = cum2[rows]
        tab_ref[k, TAB_ROW2] = row2[rows]
        tab_ref[k, TAB_COEF] = coef[rows]
        tab_ref[k, TAB_E] = e[rows]
        tab_ref[k, TAB_CD] = cd[rows]


def _ssd_scan_kernel(sz_ref, x_ref, b_ref, c_ref, tab_ref, dcol_ref, gcol_ref,
                     *rest, seq_chunks, has_h0, has_prev, emit_state, unroll):
    rest = list(rest)
    h0_ref = rest.pop(0) if has_h0 else None
    prev_ref = rest.pop(0) if has_prev else None
    y_ref = rest.pop(0)
    st_ref = rest.pop(0) if emit_state else None
    yp_s, stf_s, stb_s, hf_s, hb_s, seg_s, cb_s, bm_s = rest

    nc, gw, _ = x_ref.shape
    n = b_ref.shape[1]
    hp = gw // HEADS_PER_GROUP
    nh = HEADS_PER_GROUP
    r2 = 2 * nh
    assert not has_h0 or nc == seq_chunks

    li = lax.broadcasted_iota(jnp.int32, (CHUNK, CHUNK), 0)
    si = lax.broadcasted_iota(jnp.int32, (CHUNK, CHUNK), 1)

    def expand(rows4):
        return jnp.concatenate(
            [jnp.broadcast_to(rows4[h:h + 1, :], (hp, LANES)) for h in range(nh)], axis=0)

    blk = (lax.broadcasted_iota(jnp.int32, (r2, r2 * CHUNK), 1) // CHUNK
           == lax.broadcasted_iota(jnp.int32, (r2, r2 * CHUNK), 0))
    sel_k = jnp.concatenate([jnp.where(blk, 1.0, 0.0)] * 3, axis=0)
    ones_k = jnp.ones((3 * r2, CHUNK), F32)

    def split3(v):
        p0 = v.astype(BF16).astype(F32)
        p1 = (v - p0).astype(BF16).astype(F32)
        return p0, p1, v - p0 - p1

    def seg_chunk(c, carry):
        lhs = jnp.concatenate([-p for p in split3(tab_ref[c, TAB_ROW2])] + [ones_k], axis=0)
        rhs = jnp.concatenate(
            [sel_k] + [jnp.where(blk, jnp.concatenate([p] * r2, axis=1), 0.0)
                       for p in split3(tab_ref[c, TAB_CUM2])], axis=0)
        seg_s[c] = lax.dot_general(lhs.astype(BF16), rhs.astype(BF16), (((0,), (0,)), ((), ())),
                                   preferred_element_type=F32)
        bm = b_ref[c].T
        bm_s[c] = bm
        cb_s[c] = _dot(bm, c_ref[c])
        return carry

    lax.fori_loop(0, nc, seg_chunk, 0, unroll=unroll)

    def local_chunk(c, carry):
        xst_b = x_ref[c]
        xst = xst_b.astype(F32)
        cb = cb_s[c]
        parts = []
        for h in range(nh):
            hb = nh + h
            arg = jnp.where(li <= si, seg_s[c, :, h * CHUNK:(h + 1) * CHUNK],
                            seg_s[c, :, hb * CHUNK:(hb + 1) * CHUNK])
            w = (cb * jnp.exp2(arg)).astype(BF16)
            parts.append(_dot(xst_b[h * hp:(h + 1) * hp, :], w))
        cb_diag = jnp.sum(jnp.where(si == li, cb, 0.0), axis=0, keepdims=True)
        skip = dcol_ref[...] + expand(tab_ref[c, TAB_DT][nh:r2] * cb_diag)
        yp_s[c] = jnp.concatenate(parts, axis=0) + skip * xst

        coef8 = tab_ref[c, TAB_COEF]
        xdw = jnp.concatenate([xst * expand(coef8[0:nh]), xst * expand(coef8[nh:r2])], axis=0)
        st = _dot(xdw.astype(BF16), bm_s[c])
        stf_s[c] = st[0:gw]
        stb_s[c] = st[gw:2 * gw]
        return carry

    lax.fori_loop(0, nc, local_chunk, 0, unroll=unroll)

    if has_h0:
        h0f = h0_ref[0].reshape(gw, n)
        h0b = h0_ref[1].reshape(gw, n)
    else:
        h0f = jnp.zeros((gw, n), F32)
        h0b = h0f
    if has_prev:
        st_ref[:, 0:prev_ref.shape[1]] = prev_ref[...]
    slot = st_ref.shape[1] - 1 if emit_state else None

    def fwd_state(c, h):
        if nc > seq_chunks:
            h = jnp.where(c % seq_chunks == 0, 0.0, h)
        hf_s[c] = h.astype(BF16)
        h = h * expand(tab_ref[c, TAB_CD][0:nh]) + stf_s[c]
        if emit_state:
            @pl.when(c % seq_chunks == seq_chunks - 1)
            def _():
                st_ref[c // seq_chunks, slot, 0] = h.reshape(nh, hp, n)
        return h

    def bwd_state(c, h):
        if nc > seq_chunks:
            h = jnp.where(c % seq_chunks == seq_chunks - 1, 0.0, h)
        hb_s[c] = h.astype(BF16)
        h = h * expand(tab_ref[c, TAB_CD][nh:r2]) + stb_s[c]
        if emit_state:
            @pl.when(c % seq_chunks == 0)
            def _():
                st_ref[c // seq_chunks, slot, 1] = h.reshape(nh, hp, n)
        return h

    lax.fori_loop(0, nc, fwd_state, h0f)
    lax.fori_loop(0, nc, lambda i, h: bwd_state(nc - 1 - i, h), h0b)

    def output_chunk(c, carry):
        hin = jnp.concatenate([hf_s[c], hb_s[c]], axis=0)
        yo = _dot(hin, c_ref[c])
        e8 = tab_ref[c, TAB_E]
        y_t = yp_s[c] + yo[0:gw] * expand(e8[0:nh]) + yo[gw:2 * gw] * expand(e8[nh:r2])
        y = y_t * sz_ref[c].astype(F32)
        inv = lax.rsqrt(jnp.mean(y * y, axis=0, keepdims=True) + EPS)
        y_ref[c] = (y * inv * gcol_ref[...]).astype(BF16)
        return carry

    lax.fori_loop(0, nc, output_chunk, 0, unroll=unroll)


def _ssd_out_kernel(y_ref, w_ref, x_ref, mod_ref, *rest, final):
    fg_ref = rest[0] if final else None
    o_ref = rest[-1]
    y = jnp.concatenate([y_ref[k].T for k in range(y_ref.shape[0])], axis=0)
    out = _dot(y, w_ref[...])
    o_ref[...] = _residual(x_ref[...], out, mod_ref, fg_ref)


def _ssd_layer(x2d, seq, mod3, mod_row, norm_g, w_zxbc, wdt_t, dt_bias_col, alog_col,
               conv_wb, d_col, gn_g, w_out, h0, prev_state, layer_idx, emit_state, final_g):
    t, d = x2d.shape
    d_inner = w_out.shape[1]
    heads2 = wdt_t.shape[1]
    groups = heads2 // (2 * HEADS_PER_GROUP)
    gw = d_inner // groups
    n = (w_zxbc.shape[1] - 2 * d_inner) // (2 * groups)
    hp = gw // HEADS_PER_GROUP
    ln = seq.length
    nc = ln // CHUNK
    tm = min(TOKEN_TILE, t)
    tiles_per_seq = max(ln // tm, 1)
    row_of_tile = lambda i: mod_row(i // tiles_per_seq)
    const = lambda *_: (0, 0)
    tok = pl.BlockSpec((tm, d), lambda i: (i, 0))
    wrows = w_zxbc.shape[1]

    nt = ln * max(1, min(SSD_IN_TOKENS, t) // ln) if seq.shared_cond else ln
    seqs_per_block = nt // ln
    rows_blk = min(SSD_IN_ROWS, d_inner)
    assert t % nt == 0 and d_inner % rows_blk == 0 and wrows % rows_blk == 0
    n_tab = TAB_CD + 1
    gate_blocks = d_inner // rows_blk
    proj, tab = pl.pallas_call(
        functools.partial(_ssd_in_kernel, seq_len=ln, gate_blocks=gate_blocks,
                          sub_rows=min(SSD_IN_SUB_ROWS, rows_blk)),
        grid=(t // nt, wrows // rows_blk),
        in_specs=[
            pl.BlockSpec((nt, d), lambda i, j: (i, 0)),
            pl.BlockSpec((1, d), const),
            pl.BlockSpec((None, 1, 3 * d), lambda i, j: (mod_row(i * seqs_per_block), 0, 0)),
            pl.BlockSpec((None, rows_blk, d), lambda i, j: (layer_idx, j, 0)),
            pl.BlockSpec((None, heads2, d), lambda i, j: (layer_idx, 0, 0)),
            pl.BlockSpec((heads2, 1), const),
            pl.BlockSpec((heads2, 1), const),
            pl.BlockSpec((rows_blk, 4), lambda i, j: (jnp.maximum(j - gate_blocks, 0), 0)),
        ],
        out_specs=[
            pl.BlockSpec((nt // CHUNK, rows_blk, CHUNK), lambda i, j: (i, j, 0)),
            pl.BlockSpec((nt // CHUNK, n_tab, heads2, CHUNK), lambda i, j: (i, 0, 0, 0)),
        ],
        out_shape=[
            jax.ShapeDtypeStruct((t // CHUNK, wrows, CHUNK), BF16),
            jax.ShapeDtypeStruct((t // CHUNK, n_tab, heads2, CHUNK), F32),
        ],
        scratch_shapes=[pltpu.VMEM((nt, d), BF16)],
        compiler_params=_params(2),
        name="ssd_in",
    )(x2d, norm_g[None], mod3, w_zxbc, wdt_t, dt_bias_col, alog_col, conv_wb)

    r2 = 2 * HEADS_PER_GROUP
    xg0 = d_inner // gw
    bg0 = 2 * d_inner // n
    cg0 = bg0 + groups
    has_h0 = h0 is not None
    has_prev = emit_state and prev_state is not None
    spc = 1 if has_h0 else max(1, min(seq.count, SCAN_CELL_CHUNKS // nc))
    assert seq.count % spc == 0
    cc = spc * nc
    nh = HEADS_PER_GROUP
    in_specs = [
        pl.BlockSpec((cc, gw, CHUNK), lambda s, g: (s, g, 0)),
        pl.BlockSpec((cc, gw, CHUNK), lambda s, g: (s, xg0 + g, 0)),
        pl.BlockSpec((cc, n, CHUNK), lambda s, g: (s, bg0 + g, 0)),
        pl.BlockSpec((cc, n, CHUNK), lambda s, g: (s, cg0 + g, 0)),
        pl.BlockSpec((cc, n_tab, r2, CHUNK), lambda s, g: (s, 0, g, 0)),
        pl.BlockSpec((gw, LANES), lambda s, g: (g, 0)),
        pl.BlockSpec((gw, LANES), lambda s, g: (g, 0)),
    ]
    args = [proj, proj, proj, proj, tab, jnp.broadcast_to(d_col, (d_inner, LANES)),
            jnp.broadcast_to(gn_g[:, None], (d_inner, LANES))]
    if has_h0:
        in_specs.append(pl.BlockSpec((None, None, 2, nh, hp, n),
                                     lambda s, g: (s, layer_idx, 0, g, 0, 0)))
        args.append(h0)
    if has_prev:
        in_specs.append(pl.BlockSpec((spc, prev_state.shape[1], 2, nh, hp, n),
                                     lambda s, g: (s, 0, 0, g, 0, 0)))
        args.append(prev_state)
    out_specs = [pl.BlockSpec((cc, gw, CHUNK), lambda s, g: (s, g, 0))]
    out_shape = [jax.ShapeDtypeStruct((t // CHUNK, d_inner, CHUNK), BF16)]
    if emit_state:
        slots = (prev_state.shape[1] if has_prev else 0) + 1
        out_specs.append(pl.BlockSpec((spc, slots, 2, nh, hp, n), lambda s, g: (s, 0, 0, g, 0, 0)))
        out_shape.append(jax.ShapeDtypeStruct((seq.count, slots, 2, groups * nh, hp, n), F32))
    res = pl.pallas_call(
        functools.partial(_ssd_scan_kernel, seq_chunks=nc, has_h0=has_h0, has_prev=has_prev,
                          emit_state=emit_state, unroll=min(16, cc)),
        grid=(seq.count // spc, groups),
        in_specs=in_specs,
        out_specs=out_specs,
        out_shape=out_shape,
        scratch_shapes=[
            pltpu.VMEM((cc, gw, CHUNK), F32),
            pltpu.VMEM((cc, gw, n), F32),
            pltpu.VMEM((cc, gw, n), F32),
            pltpu.VMEM((cc, gw, n), BF16),
            pltpu.VMEM((cc, gw, n), BF16),
            pltpu.VMEM((cc, CHUNK, r2 * CHUNK), F32),
            pltpu.VMEM((cc, CHUNK, CHUNK), F32),
            pltpu.VMEM((cc, CHUNK, n), BF16),
        ],
        compiler_params=_params(2),
        name="ssd_scan",
    )(*args)
    y, state = (res[0], res[1]) if emit_state else (res[0], None)

    final = final_g is not None
    in_specs = [
        pl.BlockSpec((tm // CHUNK, d_inner, CHUNK), lambda i: (i, 0, 0)),
        pl.BlockSpec((None, d_inner, d), lambda i: (layer_idx, 0, 0), pipeline_mode=pl.Buffered(1)),
        tok,
        pl.BlockSpec((None, 1, 3 * d), lambda i: (row_of_tile(i), 0, 0)),
    ]
    args = [y, w_out, x2d, mod3]
    if final:
        in_specs.append(pl.BlockSpec((1, d), const))
        args.append(final_g[None])
    x_new = pl.pallas_call(
        functools.partial(_ssd_out_kernel, final=final),
        grid=(t // tm,),
        in_specs=in_specs,
        out_specs=tok,
        out_shape=jax.ShapeDtypeStruct((t, d), F32),
        compiler_params=_params(1),
        name="ssd_out",
    )(*args)
    return x_new, state


class _Seqs:
    def __init__(self, count, length, shared_cond):
        self.count = count
        self.length = length
        self.shared_cond = shared_cond


def _trunk(x, mod, mod_row, shared_cond, row_len, h0, emit_state, weights):
    (norm_g, fc_w_in, fc_conv_w, fc_w_out, ssd_w, ssd_wdt_t, ssd_dtb, ssd_alog, ssd_conv_wb,
     ssd_dcol, ssd_norm_g, ssd_w_out, final_norm_g) = weights
    bsz, ln, d = x.shape
    seq = _Seqs(bsz, ln, shared_cond)
    x2d = x.reshape(bsz * ln, d)
    depth = norm_g.shape[0]
    states = None
    for layer in range(depth):
        i = layer // 2
        mod3 = mod[layer].reshape(COND_ROWS, 1, 3 * d)
        final_g = final_norm_g if layer == depth - 1 else None
        if layer % 2 == 0:
            x2d = _fc_layer(x2d, seq, mod3, mod_row, norm_g[layer], fc_w_in, fc_conv_w[i],
                            fc_w_out, i, row_len, final_g)
        else:
            x2d, st = _ssd_layer(x2d, seq, mod3, mod_row, norm_g[layer], ssd_w, ssd_wdt_t,
                                 ssd_dtb[i], ssd_alog[i], ssd_conv_wb[i], ssd_dcol[i],
                                 ssd_norm_g[i], ssd_w_out, h0, states, i, emit_state, final_g)
            states = st
    return x2d.reshape(bsz, ln, d), states


def kernel(x_prompt, x_sample, state_ssm, c, c_ctx, w_mod, b_mod, norm_g, fc_w_in, fc_conv_w,
           fc_w_out, ssd_w_in, ssd_conv_w, ssd_conv_b, ssd_dt_bias, ssd_a_log, ssd_d,
           ssd_norm_g, ssd_w_out, final_norm_g):
    d = x_prompt.shape[-1]
    dec_batch = x_sample.shape[0]
    no, _, heads = ssd_a_log.shape
    d_inner = ssd_w_out.shape[1]
    conv_dim = ssd_conv_w.shape[-1]
    hp = d_inner // heads
    groups = heads // HEADS_PER_GROUP
    assert dec_batch + 1 <= COND_ROWS and heads % HEADS_PER_GROUP == 0

    cond = jnp.zeros((COND_ROWS, d), F32).at[:dec_batch].set(c).at[dec_batch].set(c_ctx)
    mod = _modulation(cond, w_mod, b_mod)

    zx = d_inner + conv_dim
    ssd_w = jnp.swapaxes(ssd_w_in[:, :, :zx], 1, 2).astype(BF16)
    ssd_conv_wb = jnp.swapaxes(jnp.concatenate([ssd_conv_w, ssd_conv_b[:, None]], axis=1), 1, 2)
    perm = jnp.arange(2 * heads).reshape(2, groups, HEADS_PER_GROUP).transpose(1, 0, 2).reshape(-1)
    ssd_wdt_t = jnp.swapaxes(ssd_w_in[:, :, zx:][:, :, perm], 1, 2).astype(BF16)
    ssd_dtb = ssd_dt_bias.reshape(no, 2 * heads)[:, perm][:, :, None]
    ssd_alog = ssd_a_log.reshape(no, 2 * heads)[:, perm][:, :, None]
    ssd_dcol = jnp.repeat(ssd_d, hp, axis=1)[:, :, None]
    weights = (norm_g, fc_w_in.astype(BF16), fc_conv_w, fc_w_out.astype(BF16), ssd_w, ssd_wdt_t,
               ssd_dtb, ssd_alog, ssd_conv_wb, ssd_dcol, ssd_norm_g,
               ssd_w_out.astype(BF16), final_norm_g)

    y_prompt, states = _trunk(x_prompt, mod, lambda s: dec_batch, True, x_prompt.shape[1], None,
                              True, weights)
    y_sample, _ = _trunk(x_sample, mod, lambda s: s, False, GRID_W, state_ssm, False, weights)
    return y_prompt, y_sample, states
```

```python
import functools
import math

import jax
import jax.numpy as jnp
from jax import lax
from jax.experimental import pallas as pl
from jax.experimental.pallas import tpu as pltpu

F32 = jnp.float32
BF16 = jnp.bfloat16

EPS = 1e-6
GRID_W = 64
FOURIER_GROUPS = 4
DFT_INNER = 128
CHUNK = 128
HEADS_PER_GROUP = 4
LANES = 128
TOKEN_TILE = 512
FC_IN_TILE = 1024
SSD_OUT_TILE = 1024
SSD_IN_TOKENS = 2048
SSD_IN_ROWS = 1024
SSD_IN_SUB_ROWS = 256
SCAN_CELL_CHUNKS = 16
COND_ROWS = 16
VMEM_LIMIT = 56 * 1024 * 1024


def _params(n_grid):
    return pltpu.CompilerParams(
        dimension_semantics=("arbitrary",) * n_grid, vmem_limit_bytes=VMEM_LIMIT)


def _dot(a, b):
    return jnp.dot(a, b, preferred_element_type=F32)


def _dot_nt(a, b):
    return lax.dot_general(a, b, (((1,), (1,)), ((), ())), preferred_element_type=F32)


def _sigmoid(x):
    return 1.0 / (1.0 + jnp.exp(-x))


def _silu(x):
    return x * _sigmoid(x)


def _rms(x):
    return x * lax.rsqrt(jnp.mean(x * x, axis=-1, keepdims=True) + EPS)


def _modnorm(x, g_ref, mod_ref):
    d = x.shape[-1]
    shift = mod_ref[:, 0:d]
    scale = mod_ref[:, d:2 * d]
    return _rms(x) * g_ref[...] * (1.0 + scale) + shift


def _residual(x, out, mod_ref, fg_ref):
    d = x.shape[-1]
    xn = x + mod_ref[:, 2 * d:3 * d] * out
    if fg_ref is not None:
        xn = _rms(xn) * fg_ref[...]
    return xn


def _mod_kernel(c_ref, w_ref, b_ref, o_ref):
    act = _silu(c_ref[...]).astype(BF16)
    o_ref[...] = _dot(act, w_ref[...].astype(BF16)) + b_ref[...]


def _modulation(cond, w_mod, b_mod):
    depth, d, d3 = w_mod.shape
    nb = d3 // d
    return pl.pallas_call(
        _mod_kernel,
        grid=(depth, nb),
        in_specs=[
            pl.BlockSpec((COND_ROWS, d), lambda l, j: (0, 0)),
            pl.BlockSpec((None, d, d), lambda l, j: (l, 0, j)),
            pl.BlockSpec((None, 1, d), lambda l, j: (l, 0, j)),
        ],
        out_specs=pl.BlockSpec((None, COND_ROWS, d), lambda l, j: (l, 0, j)),
        out_shape=jax.ShapeDtypeStruct((depth, COND_ROWS, d3), F32),
        compiler_params=_params(2),
        name="modulation",
    )(cond, w_mod, b_mod.reshape(depth, 1, d3))


def _fc_in_kernel(x_ref, g_ref, mod_ref, w_ref, cw_ref, dft_ref,
                  xc_ref, xs_ref, ga_ref, yb_ref, *, row_len, col_block):
    tm, d = x_ref.shape
    gd = d // FOURIER_GROUPS
    hm = _modnorm(x_ref[...], g_ref, mod_ref).astype(BF16)
    ua = _dot(hm, w_ref[:, 0:d]).astype(BF16)
    for g in range(FOURIER_GROUPS):
        t = _dot(ua[:, g * gd:(g + 1) * gd], dft_ref[...])
        xc_ref[:, g * gd:(g + 1) * gd] = t[:, :gd].astype(BF16)
        xs_ref[:, g * gd:(g + 1) * gd] = t[:, gd:].astype(BF16)
    pos = lax.rem(lax.broadcasted_iota(jnp.int32, (tm, 1), 0), row_len)
    first = pos == 0
    last = pos == row_len - 1
    for j in range(d // col_block):
        lo = j * col_block
        za = _dot(hm, w_ref[:, d + lo:d + lo + col_block])
        ga_ref[:, lo:lo + col_block] = _silu(za).astype(BF16)
        bb = _dot(hm, w_ref[:, 2 * d + lo:2 * d + lo + col_block])
        cc = _dot(hm, w_ref[:, 3 * d + lo:3 * d + lo + col_block])
        vv = _dot(hm, w_ref[:, 4 * d + lo:4 * d + lo + col_block])
        zb = _dot(hm, w_ref[:, 5 * d + lo:5 * d + lo + col_block])
        u = cc * vv
        up = jnp.where(first, 0.0, pltpu.roll(u, 1, 0))
        un = jnp.where(last, 0.0, pltpu.roll(u, tm - 1, 0))
        cw = cw_ref[:, lo:lo + col_block]
        y = up * cw[0:1] + u * cw[1:2] + un * cw[2:3]
        yb_ref[:, lo:lo + col_block] = (bb * y * _silu(zb)).astype(BF16)


def _fc_out_kernel(dc_ref, ds_ref, rev_ref, xc_ref, xs_ref, ga_ref, yb_ref, w_ref, x_ref, mod_ref,
                   *rest, final):
    fg_ref = rest[0] if final else None
    o_ref, xcf_s, xsf_s = rest[-3:]
    tr, d = x_ref.shape
    half = rev_ref.shape[0]

    @pl.when(pl.program_id(1) == 0)
    def _():
        xcf_s[...] = (xc_ref[0:half, :].astype(F32)
                      + _dot(rev_ref[...], xc_ref[half:2 * half, :])).astype(BF16)
        xsf_s[...] = (xs_ref[0:half, :].astype(F32)
                      - _dot(rev_ref[...], xs_ref[half:2 * half, :])).astype(BF16)

    odd = lax.broadcasted_iota(jnp.int32, (tr, 1), 0) & 1
    alt = jnp.where(odd == 1, -1.0, 1.0) * (1.0 / math.sqrt(2 * half))
    ya = (_dot(dc_ref[...], xcf_s[...]) + _dot(ds_ref[...], xsf_s[...])
          + alt * xc_ref[half:half + 1, :].astype(F32))
    h1 = (ya * ga_ref[...].astype(F32)).astype(BF16)
    out = _dot(h1, w_ref[0:d, :]) + _dot(yb_ref[...], w_ref[d:2 * d, :])
    o_ref[...] = _residual(x_ref[...], out, mod_ref, fg_ref)


def _dft_tables(n, dtype=BF16):
    nb = DFT_INNER if n % DFT_INNER == 0 else 1
    k = jnp.arange(n, dtype=jnp.int32)[:, None]

    def base(cols):
        ang = ((k * cols[None, :]) % n).astype(F32) * (2.0 * math.pi / n)
        return jnp.cos(ang), jnp.sin(ang)

    ca, sa = base(jnp.arange(n // nb, dtype=jnp.int32) * nb)
    cb, sb = base(jnp.arange(nb, dtype=jnp.int32))
    ca, sa, cb, sb = ca[:, :, None], sa[:, :, None], cb[:, None, :], sb[:, None, :]
    s = 1.0 / math.sqrt(n)
    cos = ((ca * cb - sa * sb) * s).reshape(n, n)
    sin = ((sa * cb + ca * sb) * s).reshape(n, n)
    return cos.astype(dtype), sin.astype(dtype)


def _token_tile(seq, t, cap):
    tile = min(cap, t if seq.shared_cond else seq.length)
    assert t % tile == 0 and (seq.length % tile == 0 or tile % seq.length == 0)
    return tile


def _fc_layer(x2d, seq, mod3, mod_row, norm_g, w_in, conv_w, w_out, li, row_len, final_g):
    t, d = x2d.shape
    tm = _token_tile(seq, t, FC_IN_TILE)
    assert tm % row_len == 0
    tiles_per_seq = max(seq.length // tm, 1)
    gd = d // FOURIER_GROUPS
    cc, sc = _dft_tables(gd)
    dft_ch = jnp.concatenate([cc, sc], axis=1)
    row_of_tile = lambda i: mod_row(i // tiles_per_seq)
    const = lambda *_: (0, 0)
    tok = pl.BlockSpec((tm, d), lambda i: (i, 0))
    xc, xs, ga, yb = pl.pallas_call(
        functools.partial(_fc_in_kernel, row_len=row_len, col_block=min(256, d)),
        grid=(t // tm,),
        in_specs=[
            tok,
            pl.BlockSpec((1, d), const),
            pl.BlockSpec((None, 1, 3 * d), lambda i: (row_of_tile(i), 0, 0)),
            pl.BlockSpec((None, d, 6 * d), lambda i: (li, 0, 0), pipeline_mode=pl.Buffered(1)),
            pl.BlockSpec((3, d), const),
            pl.BlockSpec((gd, 2 * gd), const),
        ],
        out_specs=[tok, tok, tok, tok],
        out_shape=[jax.ShapeDtypeStruct((t, d), BF16)] * 4,
        compiler_params=_params(1),
        name="fc_in",
    )(x2d, norm_g[None], mod3, w_in, conv_w, dft_ch)

    ln = seq.length
    tr = min(TOKEN_TILE, ln)
    rt = ln // tr
    cl, sl = _dft_tables(ln)
    final = final_g is not None
    row = pl.BlockSpec((tr, d), lambda s, r: (s * rt + r, 0))
    whole = pl.BlockSpec((ln, d), lambda s, r: (s, 0))
    half = ln // 2
    assert ln % 2 == 0 and tr % 2 == 0
    ni = jnp.arange(half, dtype=jnp.int32)
    rev = jnp.where((ni[:, None] >= 1) & (ni[None, :] == half - ni[:, None]), 1.0, 0.0).astype(BF16)
    in_specs = [
        pl.BlockSpec((tr, half), lambda s, r: (r, 0)),
        pl.BlockSpec((tr, half), lambda s, r: (r, 0)),
        pl.BlockSpec((half, half), lambda s, r: (0, 0), pipeline_mode=pl.Buffered(1)),
        whole, whole, row, row,
        pl.BlockSpec((None, 2 * d, d), lambda s, r: (li, 0, 0), pipeline_mode=pl.Buffered(1)),
        row,
        pl.BlockSpec((None, 1, 3 * d), lambda s, r: (mod_row(s), 0, 0)),
    ]
    args = [cl, -sl, rev, xc, xs, ga, yb, w_out, x2d, mod3]
    if final:
        in_specs.append(pl.BlockSpec((1, d), lambda s, r: (0, 0)))
        args.append(final_g[None])
    return pl.pallas_call(
        functools.partial(_fc_out_kernel, final=final),
        grid=(seq.count, rt),
        in_specs=in_specs,
        out_specs=row,
        out_shape=jax.ShapeDtypeStruct((t, d), F32),
        scratch_shapes=[pltpu.VMEM((half, d), BF16), pltpu.VMEM((half, d), BF16)],
        compiler_params=_params(2),
        name="fc_out",
    )(*args)


TAB_DT, TAB_CUM2, TAB_ROW2, TAB_COEF, TAB_E, TAB_CD = range(6)
LOG2_E = math.log2(math.e)
LOG2_FLOOR = -1e30


def _ssd_in_kernel(x_ref, g_ref, mod_ref, wt_ref, wdt_ref, dtb_ref, alog_ref, cw_ref,
                   proj_ref, tab_ref, hm_s, *, seq_len, gate_blocks, sub_rows):
    j = pl.program_id(1)
    nt = x_ref.shape[0]
    rows = wt_ref.shape[0]
    nc = nt // CHUNK

    def emit(r0, y):
        yb = y.astype(BF16)
        for k in range(nc):
            proj_ref[k, pl.ds(r0, sub_rows), :] = yb[:, k * CHUNK:(k + 1) * CHUNK]

    @pl.when(j == 0)
    def _():
        hm = _modnorm(x_ref[...], g_ref, mod_ref).astype(BF16)
        hm_s[...] = hm
        _scan_tables(hm, wdt_ref, dtb_ref, alog_ref, tab_ref)

    @pl.when(j < gate_blocks)
    def _():
        for r0 in range(0, rows, sub_rows):
            emit(r0, _silu(_dot_nt(wt_ref[r0:r0 + sub_rows, :], hm_s[...])))

    @pl.when(j >= gate_blocks)
    def _():
        lane0 = lax.broadcasted_iota(jnp.int32, (sub_rows, LANES), 1)

        @pl.loop(0, rows // sub_rows)
        def _(i):
            r0 = pl.multiple_of(i * sub_rows, sub_rows)
            p = _dot_nt(wt_ref[pl.ds(r0, sub_rows), :], hm_s[...])
            left = pltpu.roll(p, 1, 1)
            right = pltpu.roll(p, nt - 1, 1)
            lcols = [left[:, q:q + LANES] for q in range(0, nt, LANES)]
            rcols = [right[:, q:q + LANES] for q in range(0, nt, LANES)]
            for q in range(0, nt, seq_len):
                lcols[q // LANES] = jnp.where(lane0 == 0, 0.0, lcols[q // LANES])
                e = (q + seq_len) // LANES - 1
                rcols[e] = jnp.where(lane0 == LANES - 1, 0.0, rcols[e])
            left = jnp.concatenate(lcols, axis=1)
            right = jnp.concatenate(rcols, axis=1)
            cw = cw_ref[pl.ds(r0, sub_rows), :]
            y = left * cw[:, 0:1] + p * cw[:, 1:2] + right * cw[:, 2:3] + cw[:, 3:4]
            emit(r0, _silu(y))


def _scan_tables(hm, wdt_ref, dtb_ref, alog_ref, tab_ref):
    v = _dot_nt(wdt_ref[...], hm) + dtb_ref[...]
    dt_all = jnp.maximum(v, 0.0) + jnp.log(1.0 + jnp.exp(-jnp.abs(v)))
    heads2, nt = dt_all.shape
    nc = nt // CHUNK
    la_all = dt_all * -jnp.exp(alog_ref[...])
    dt = jnp.concatenate([dt_all[:, k * CHUNK:(k + 1) * CHUNK] for k in range(nc)], axis=0)
    la = jnp.concatenate([la_all[:, k * CHUNK:(k + 1) * CHUNK] for k in range(nc)], axis=0)
    p0 = la.astype(BF16)
    r1 = la - p0.astype(F32)
    p1 = r1.astype(BF16)
    p2 = (r1 - p1.astype(F32)).astype(BF16)
    ji = lax.broadcasted_iota(jnp.int32, (CHUNK, CHUNK), 0)
    li = lax.broadcasted_iota(jnp.int32, (CHUNK, CHUNK), 1)
    upto = jnp.where(ji <= li, 1.0, 0.0).astype(BF16)
    from_ = jnp.where(ji >= li, 1.0, 0.0).astype(BF16)
    pre = _dot(p0, upto) + _dot(p1, upto) + _dot(p2, upto)
    suf = _dot(p0, from_) + _dot(p1, from_) + _dot(p2, from_)
    row = lax.broadcasted_iota(jnp.int32, la.shape, 0)
    is_fwd = (row & (2 * HEADS_PER_GROUP - 1)) < HEADS_PER_GROUP
    cum = jnp.where(is_fwd, pre, suf)
    tot = jnp.where(is_fwd, jnp.broadcast_to(cum[:, CHUNK - 1:CHUNK], cum.shape),
                    jnp.broadcast_to(cum[:, 0:1], cum.shape))
    coef = dt * jnp.exp(tot - cum)
    e = jnp.exp(cum)
    cd = jnp.exp(tot)
    cum2 = cum * LOG2_E
    row2 = cum2 - jnp.maximum(jnp.log2(dt), LOG2_FLOOR)
    for k in range(nc):
        rows = slice(k * heads2, (k + 1) * heads2)
        tab_ref[k, TAB_DT] = dt[rows]
        tab_ref[k, TAB_CUM2] = cum2[rows]
        tab_ref[k, TAB_ROW2] = row2[rows]
        tab_ref[k, TAB_COEF] = coef[rows]
        tab_ref[k, TAB_E] = e[rows]
        tab_ref[k, TAB_CD] = cd[rows]


def _ssd_scan_kernel(sz_ref, x_ref, b_ref, c_ref, tab_ref, dcol_ref, gcol_ref,
                     *rest, seq_chunks, has_h0, has_prev, emit_state, unroll):
    rest = list(rest)
    h0_ref = rest.pop(0) if has_h0 else None
    prev_ref = rest.pop(0) if has_prev else None
    y_ref = rest.pop(0)
    st_ref = rest.pop(0) if emit_state else None
    yp_s, stf_s, stb_s, hf_s, hb_s, seg_s, cb_s, bm_s = rest

    nc, gw, _ = x_ref.shape
    n = b_ref.shape[1]
    hp = gw // HEADS_PER_GROUP
    nh = HEADS_PER_GROUP
    r2 = 2 * nh
    assert not has_h0 or nc == seq_chunks

    li = lax.broadcasted_iota(jnp.int32, (CHUNK, CHUNK), 0)
    si = lax.broadcasted_iota(jnp.int32, (CHUNK, CHUNK), 1)

    def expand(rows4):
        return jnp.concatenate(
            [jnp.broadcast_to(rows4[h:h + 1, :], (hp, LANES)) for h in range(nh)], axis=0)

    blk = (lax.broadcasted_iota(jnp.int32, (r2, r2 * CHUNK), 1) // CHUNK
           == lax.broadcasted_iota(jnp.int32, (r2, r2 * CHUNK), 0))
    sel_k = jnp.concatenate([jnp.where(blk, 1.0, 0.0)] * 3, axis=0)
    ones_k = jnp.ones((3 * r2, CHUNK), F32)

    def split3(v):
        p0 = v.astype(BF16).astype(F32)
        p1 = (v - p0).astype(BF16).astype(F32)
        return p0, p1, v - p0 - p1

    def seg_chunk(c, carry):
        lhs = jnp.concatenate([-p for p in split3(tab_ref[c, TAB_ROW2])] + [ones_k], axis=0)
        rhs = jnp.concatenate(
            [sel_k] + [jnp.where(blk, jnp.concatenate([p] * r2, axis=1), 0.0)
                       for p in split3(tab_ref[c, TAB_CUM2])], axis=0)
        seg_s[c] = lax.dot_general(lhs.astype(BF16), rhs.astype(BF16), (((0,), (0,)), ((), ())),
                                   preferred_element_type=F32)
        bm = b_ref[c].T
        bm_s[c] = bm
        cb_s[c] = _dot(bm, c_ref[c])
        return carry

    lax.fori_loop(0, nc, seg_chunk, 0, unroll=unroll)

    def local_chunk(c, carry):
        xst_b = x_ref[c]
        xst = xst_b.astype(F32)
        cb = cb_s[c]
        parts = []
        for h in range(nh):
            hb = nh + h
            arg = jnp.where(li <= si, seg_s[c, :, h * CHUNK:(h + 1) * CHUNK],
                            seg_s[c, :, hb * CHUNK:(hb + 1) * CHUNK])
            w = (cb * jnp.exp2(arg)).astype(BF16)
            parts.append(_dot(xst_b[h * hp:(h + 1) * hp, :], w))
        cb_diag = jnp.sum(jnp.where(si == li, cb, 0.0), axis=0, keepdims=True)
        skip = dcol_ref[...] + expand(tab_ref[c, TAB_DT][nh:r2] * cb_diag)
        yp_s[c] = jnp.concatenate(parts, axis=0) + skip * xst

        coef8 = tab_ref[c, TAB_COEF]
        xdw = jnp.concatenate([xst * expand(coef8[0:nh]), xst * expand(coef8[nh:r2])], axis=0)
        st = _dot(xdw.astype(BF16), bm_s[c])
        stf_s[c] = st[0:gw]
        stb_s[c] = st[gw:2 * gw]
        return carry

    lax.fori_loop(0, nc, local_chunk, 0, unroll=unroll)

    if has_h0:
        h0f = h0_ref[0].reshape(gw, n)
        h0b = h0_ref[1].reshape(gw, n)
    else:
        h0f = jnp.zeros((gw, n), F32)
        h0b = h0f
    if has_prev:
        st_ref[:, 0:prev_ref.shape[1]] = prev_ref[...]
    slot = st_ref.shape[1] - 1 if emit_state else None

    def fwd_state(c, h):
        hf_s[c] = h.astype(BF16)
        return h * expand(tab_ref[c, TAB_CD][0:nh]) + stf_s[c]

    def bwd_state(c, h):
        hb_s[c] = h.astype(BF16)
        return h * expand(tab_ref[c, TAB_CD][nh:r2]) + stb_s[c]

    for q in range(nc // seq_chunks):
        c0, c1 = q * seq_chunks, (q + 1) * seq_chunks
        hf = lax.fori_loop(c0, c1, fwd_state, h0f)
        hb = lax.fori_loop(0, seq_chunks, lambda i, h: bwd_state(c1 - 1 - i, h), h0b)
        if emit_state:
            st_ref[q, slot, 0] = hf.reshape(nh, hp, n)
            st_ref[q, slot, 1] = hb.reshape(nh, hp, n)

    def output_chunk(c, carry):
        hin = jnp.concatenate([hf_s[c], hb_s[c]], axis=0)
        yo = _dot(hin, c_ref[c])
        e8 = tab_ref[c, TAB_E]
        y_t = yp_s[c] + yo[0:gw] * expand(e8[0:nh]) + yo[gw:2 * gw] * expand(e8[nh:r2])
        y = y_t * sz_ref[c].astype(F32)
        inv = lax.rsqrt(jnp.mean(y * y, axis=0, keepdims=True) + EPS)
        y_ref[c] = (y * inv * gcol_ref[...]).astype(BF16)
        return carry

    lax.fori_loop(0, nc, output_chunk, 0, unroll=unroll)


def _ssd_out_kernel(y_ref, w_ref, x_ref, mod_ref, *rest, final):
    fg_ref = rest[0] if final else None
    o_ref = rest[-1]
    y = jnp.concatenate([y_ref[k].T for k in range(y_ref.shape[0])], axis=0)
    out = _dot(y, w_ref[...])
    o_ref[...] = _residual(x_ref[...], out, mod_ref, fg_ref)


def _ssd_layer(x2d, seq, mod3, mod_row, norm_g, w_zxbc, wdt_t, dt_bias_col, alog_col,
               conv_wb, d_col, gn_g, w_out, h0, prev_state, layer_idx, emit_state, final_g):
    t, d = x2d.shape
    d_inner = w_out.shape[1]
    heads2 = wdt_t.shape[1]
    groups = heads2 // (2 * HEADS_PER_GROUP)
    gw = d_inner // groups
    n = (w_zxbc.shape[1] - 2 * d_inner) // (2 * groups)
    hp = gw // HEADS_PER_GROUP
    ln = seq.length
    nc = ln // CHUNK
    tm = _token_tile(seq, t, SSD_OUT_TILE)
    tiles_per_seq = max(ln // tm, 1)
    row_of_tile = lambda i: mod_row(i // tiles_per_seq)
    const = lambda *_: (0, 0)
    tok = pl.BlockSpec((tm, d), lambda i: (i, 0))
    wrows = w_zxbc.shape[1]

    nt = ln * max(1, min(SSD_IN_TOKENS, t) // ln) if seq.shared_cond else ln
    seqs_per_block = nt // ln
    rows_blk = min(SSD_IN_ROWS, d_inner)
    assert t % nt == 0 and d_inner % rows_blk == 0 and wrows % rows_blk == 0
    n_tab = TAB_CD + 1
    gate_blocks = d_inner // rows_blk
    proj, tab = pl.pallas_call(
        functools.partial(_ssd_in_kernel, seq_len=ln, gate_blocks=gate_blocks,
                          sub_rows=min(SSD_IN_SUB_ROWS, rows_blk)),
        grid=(t // nt, wrows // rows_blk),
        in_specs=[
            pl.BlockSpec((nt, d), lambda i, j: (i, 0)),
            pl.BlockSpec((1, d), const),
            pl.BlockSpec((None, 1, 3 * d), lambda i, j: (mod_row(i * seqs_per_block), 0, 0)),
            pl.BlockSpec((None, rows_blk, d), lambda i, j: (layer_idx, j, 0)),
            pl.BlockSpec((None, heads2, d), lambda i, j: (layer_idx, 0, 0)),
            pl.BlockSpec((heads2, 1), const),
            pl.BlockSpec((heads2, 1), const),
            pl.BlockSpec((rows_blk, 4), lambda i, j: (jnp.maximum(j - gate_blocks, 0), 0)),
        ],
        out_specs=[
            pl.BlockSpec((nt // CHUNK, rows_blk, CHUNK), lambda i, j: (i, j, 0)),
            pl.BlockSpec((nt // CHUNK, n_tab, heads2, CHUNK), lambda i, j: (i, 0, 0, 0)),
        ],
        out_shape=[
            jax.ShapeDtypeStruct((t // CHUNK, wrows, CHUNK), BF16),
            jax.ShapeDtypeStruct((t // CHUNK, n_tab, heads2, CHUNK), F32),
        ],
        scratch_shapes=[pltpu.VMEM((nt, d), BF16)],
        compiler_params=_params(2),
        name="ssd_in",
    )(x2d, norm_g[None], mod3, w_zxbc, wdt_t, dt_bias_col, alog_col, conv_wb)

    r2 = 2 * HEADS_PER_GROUP
    xg0 = d_inner // gw
    bg0 = 2 * d_inner // n
    cg0 = bg0 + groups
    has_h0 = h0 is not None
    has_prev = emit_state and prev_state is not None
    spc = 1 if has_h0 else max(1, min(seq.count, SCAN_CELL_CHUNKS // nc))
    assert seq.count % spc == 0
    cc = spc * nc
    nh = HEADS_PER_GROUP
    in_specs = [
        pl.BlockSpec((cc, gw, CHUNK), lambda s, g: (s, g, 0)),
        pl.BlockSpec((cc, gw, CHUNK), lambda s, g: (s, xg0 + g, 0)),
        pl.BlockSpec((cc, n, CHUNK), lambda s, g: (s, bg0 + g, 0)),
        pl.BlockSpec((cc, n, CHUNK), lambda s, g: (s, cg0 + g, 0)),
        pl.BlockSpec((cc, n_tab, r2, CHUNK), lambda s, g: (s, 0, g, 0)),
        pl.BlockSpec((gw, LANES), lambda s, g: (g, 0)),
        pl.BlockSpec((gw, LANES), lambda s, g: (g, 0)),
    ]
    args = [proj, proj, proj, proj, tab, jnp.broadcast_to(d_col, (d_inner, LANES)),
            jnp.broadcast_to(gn_g[:, None], (d_inner, LANES))]
    if has_h0:
        in_specs.append(pl.BlockSpec((None, None, 2, nh, hp, n),
                                     lambda s, g: (s, layer_idx, 0, g, 0, 0)))
        args.append(h0)
    if has_prev:
        in_specs.append(pl.BlockSpec((spc, prev_state.shape[1], 2, nh, hp, n),
                                     lambda s, g: (s, 0, 0, g, 0, 0)))
        args.append(prev_state)
    out_specs = [pl.BlockSpec((cc, gw, CHUNK), lambda s, g: (s, g, 0))]
    out_shape = [jax.ShapeDtypeStruct((t // CHUNK, d_inner, CHUNK), BF16)]
    if emit_state:
        slots = (prev_state.shape[1] if has_prev else 0) + 1
        out_specs.append(pl.BlockSpec((spc, slots, 2, nh, hp, n), lambda s, g: (s, 0, 0, g, 0, 0)))
        out_shape.append(jax.ShapeDtypeStruct((seq.count, slots, 2, groups * nh, hp, n), F32))
    res = pl.pallas_call(
        functools.partial(_ssd_scan_kernel, seq_chunks=nc, has_h0=has_h0, has_prev=has_prev,
                          emit_state=emit_state, unroll=min(16, cc)),
        grid=(seq.count // spc, groups),
        in_specs=in_specs,
        out_specs=out_specs,
        out_shape=out_shape,
        scratch_shapes=[
            pltpu.VMEM((cc, gw, CHUNK), F32),
            pltpu.VMEM((cc, gw, n), F32),
            pltpu.VMEM((cc, gw, n), F32),
            pltpu.VMEM((cc, gw, n), BF16),
            pltpu.VMEM((cc, gw, n), BF16),
            pltpu.VMEM((cc, CHUNK, r2 * CHUNK), F32),
            pltpu.VMEM((cc, CHUNK, CHUNK), F32),
            pltpu.VMEM((cc, CHUNK, n), BF16),
        ],
        compiler_params=_params(2),
        name="ssd_scan",
    )(*args)
    y, state = (res[0], res[1]) if emit_state else (res[0], None)

    final = final_g is not None
    in_specs = [
        pl.BlockSpec((tm // CHUNK, d_inner, CHUNK), lambda i: (i, 0, 0)),
        pl.BlockSpec((None, d_inner, d), lambda i: (layer_idx, 0, 0), pipeline_mode=pl.Buffered(1)),
        tok,
        pl.BlockSpec((None, 1, 3 * d), lambda i: (row_of_tile(i), 0, 0)),
    ]
    args = [y, w_out, x2d, mod3]
    if final:
        in_specs.append(pl.BlockSpec((1, d), const))
        args.append(final_g[None])
    x_new = pl.pallas_call(
        functools.partial(_ssd_out_kernel, final=final),
        grid=(t // tm,),
        in_specs=in_specs,
        out_specs=tok,
        out_shape=jax.ShapeDtypeStruct((t, d), F32),
        compiler_params=_params(1),
        name="ssd_out",
    )(*args)
    return x_new, state


class _Seqs:
    def __init__(self, count, length, shared_cond):
        self.count = count
        self.length = length
        self.shared_cond = shared_cond


def _trunk(x, mod, mod_row, shared_cond, row_len, h0, emit_state, weights):
    (norm_g, fc_w_in, fc_conv_w, fc_w_out, ssd_w, ssd_wdt_t, ssd_dtb, ssd_alog, ssd_conv_wb,
     ssd_dcol, ssd_norm_g, ssd_w_out, final_norm_g) = weights
    bsz, ln, d = x.shape
    seq = _Seqs(bsz, ln, shared_cond)
    x2d = x.reshape(bsz * ln, d)
    depth = norm_g.shape[0]
    states = None
    for layer in range(depth):
        i = layer // 2
        mod3 = mod[layer].reshape(COND_ROWS, 1, 3 * d)
        final_g = final_norm_g if layer == depth - 1 else None
        if layer % 2 == 0:
            x2d = _fc_layer(x2d, seq, mod3, mod_row, norm_g[layer], fc_w_in, fc_conv_w[i],
                            fc_w_out, i, row_len, final_g)
        else:
            x2d, st = _ssd_layer(x2d, seq, mod3, mod_row, norm_g[layer], ssd_w, ssd_wdt_t,
                                 ssd_dtb[i], ssd_alog[i], ssd_conv_wb[i], ssd_dcol[i],
                                 ssd_norm_g[i], ssd_w_out, h0, states, i, emit_state, final_g)
            states = st
    return x2d.reshape(bsz, ln, d), states


def kernel(x_prompt, x_sample, state_ssm, c, c_ctx, w_mod, b_mod, norm_g, fc_w_in, fc_conv_w,
           fc_w_out, ssd_w_in, ssd_conv_w, ssd_conv_b, ssd_dt_bias, ssd_a_log, ssd_d,
           ssd_norm_g, ssd_w_out, final_norm_g):
    d = x_prompt.shape[-1]
    dec_batch = x_sample.shape[0]
    no, _, heads = ssd_a_log.shape
    d_inner = ssd_w_out.shape[1]
    conv_dim = ssd_conv_w.shape[-1]
    hp = d_inner // heads
    groups = heads // HEADS_PER_GROUP
    assert dec_batch + 1 <= COND_ROWS and heads % HEADS_PER_GROUP == 0

    cond = jnp.zeros((COND_ROWS, d), F32).at[:dec_batch].set(c).at[dec_batch].set(c_ctx)
    mod = _modulation(cond, w_mod, b_mod)

    zx = d_inner + conv_dim
    ssd_w = jnp.swapaxes(ssd_w_in[:, :, :zx], 1, 2).astype(BF16)
    ssd_conv_wb = jnp.swapaxes(jnp.concatenate([ssd_conv_w, ssd_conv_b[:, None]], axis=1), 1, 2)
    perm = jnp.arange(2 * heads).reshape(2, groups, HEADS_PER_GROUP).transpose(1, 0, 2).reshape(-1)
    ssd_wdt_t = jnp.swapaxes(ssd_w_in[:, :, zx:][:, :, perm], 1, 2).astype(BF16)
    ssd_dtb = ssd_dt_bias.reshape(no, 2 * heads)[:, perm][:, :, None]
    ssd_alog = ssd_a_log.reshape(no, 2 * heads)[:, perm][:, :, None]
    ssd_dcol = jnp.repeat(ssd_d, hp, axis=1)[:, :, None]
    weights = (norm_g, fc_w_in.astype(BF16), fc_conv_w, fc_w_out.astype(BF16), ssd_w, ssd_wdt_t,
               ssd_dtb, ssd_alog, ssd_conv_wb, ssd_dcol, ssd_norm_g,
               ssd_w_out.astype(BF16), final_norm_g)

    y_prompt, states = _trunk(x_prompt, mod, lambda s: dec_batch, True, x_prompt.shape[1], None,
                              True, weights)
    y_sample, _ = _trunk(x_sample, mod, lambda s: s, False, GRID_W, state_ssm, False, weights)
    return y_prompt, y_sample, states
```

```python
import functools
import math

import jax
import jax.numpy as jnp
from jax import lax
from jax.experimental import pallas as pl
from jax.experimental.pallas import tpu as pltpu

F32 = jnp.float32
BF16 = jnp.bfloat16

EPS = 1e-6
GRID_W = 64
FOURIER_GROUPS = 4
DFT_INNER = 128
REVERSE_BLOCK = 128
CHUNK = 128
HEADS_PER_GROUP = 4
LANES = 128
TOKEN_TILE = 512
FC_IN_TILE = 1024
SSD_OUT_TILE = 1024
SSD_IN_TOKENS = 2048
SSD_IN_ROWS = 1024
SSD_IN_SUB_ROWS = 256
SCAN_CELL_CHUNKS = 16
COND_ROWS = 16
VMEM_LIMIT = 56 * 1024 * 1024


def _params(n_grid):
    return pltpu.CompilerParams(
        dimension_semantics=("arbitrary",) * n_grid, vmem_limit_bytes=VMEM_LIMIT)


def _dot(a, b):
    return jnp.dot(a, b, preferred_element_type=F32)


def _dot_nt(a, b):
    return lax.dot_general(a, b, (((1,), (1,)), ((), ())), preferred_element_type=F32)


def _sigmoid(x):
    return 1.0 / (1.0 + jnp.exp(-x))


def _silu(x):
    return x * _sigmoid(x)


def _rms(x):
    return x * lax.rsqrt(jnp.mean(x * x, axis=-1, keepdims=True) + EPS)


def _modnorm(x, g_ref, mod_ref):
    d = x.shape[-1]
    shift = mod_ref[:, 0:d]
    scale = mod_ref[:, d:2 * d]
    return _rms(x) * g_ref[...] * (1.0 + scale) + shift


def _residual(x, out, mod_ref, fg_ref):
    d = x.shape[-1]
    xn = x + mod_ref[:, 2 * d:3 * d] * out
    if fg_ref is not None:
        xn = _rms(xn) * fg_ref[...]
    return xn


def _mod_kernel(c_ref, w_ref, b_ref, o_ref):
    act = _silu(c_ref[...]).astype(BF16)
    o_ref[...] = _dot(act, w_ref[...].astype(BF16)) + b_ref[...]


def _modulation(cond, w_mod, b_mod):
    depth, d, d3 = w_mod.shape
    nb = d3 // d
    return pl.pallas_call(
        _mod_kernel,
        grid=(depth, nb),
        in_specs=[
            pl.BlockSpec((COND_ROWS, d), lambda l, j: (0, 0)),
            pl.BlockSpec((None, d, d), lambda l, j: (l, 0, j)),
            pl.BlockSpec((None, 1, d), lambda l, j: (l, 0, j)),
        ],
        out_specs=pl.BlockSpec((None, COND_ROWS, d), lambda l, j: (l, 0, j)),
        out_shape=jax.ShapeDtypeStruct((depth, COND_ROWS, d3), F32),
        compiler_params=_params(2),
        name="modulation",
    )(cond, w_mod, b_mod.reshape(depth, 1, d3))


def _fc_in_kernel(x_ref, g_ref, mod_ref, w_ref, cw_ref, dft_ref,
                  xc_ref, xs_ref, ga_ref, yb_ref, *, row_len, col_block):
    tm, d = x_ref.shape
    gd = d // FOURIER_GROUPS
    hm = _modnorm(x_ref[...], g_ref, mod_ref).astype(BF16)
    ua = _dot(hm, w_ref[:, 0:d]).astype(BF16)
    for g in range(FOURIER_GROUPS):
        t = _dot(ua[:, g * gd:(g + 1) * gd], dft_ref[...])
        xc_ref[:, g * gd:(g + 1) * gd] = t[:, :gd].astype(BF16)
        xs_ref[:, g * gd:(g + 1) * gd] = t[:, gd:].astype(BF16)
    pos = lax.rem(lax.broadcasted_iota(jnp.int32, (tm, 1), 0), row_len)
    first = pos == 0
    last = pos == row_len - 1
    for j in range(d // col_block):
        lo = j * col_block
        za = _dot(hm, w_ref[:, d + lo:d + lo + col_block])
        ga_ref[:, lo:lo + col_block] = _silu(za).astype(BF16)
        bb = _dot(hm, w_ref[:, 2 * d + lo:2 * d + lo + col_block])
        cc = _dot(hm, w_ref[:, 3 * d + lo:3 * d + lo + col_block])
        vv = _dot(hm, w_ref[:, 4 * d + lo:4 * d + lo + col_block])
        zb = _dot(hm, w_ref[:, 5 * d + lo:5 * d + lo + col_block])
        u = cc * vv
        up = jnp.where(first, 0.0, pltpu.roll(u, 1, 0))
        un = jnp.where(last, 0.0, pltpu.roll(u, tm - 1, 0))
        cw = cw_ref[:, lo:lo + col_block]
        y = up * cw[0:1] + u * cw[1:2] + un * cw[2:3]
        yb_ref[:, lo:lo + col_block] = (bb * y * _silu(zb)).astype(BF16)


def _fc_out_kernel(dc_ref, ds_ref, rev_ref, xc_ref, xs_ref, ga_ref, yb_ref, w_ref, x_ref, mod_ref,
                   *rest, final):
    fg_ref = rest[0] if final else None
    o_ref, xcf_s, xsf_s = rest[-3:]
    tr, d = x_ref.shape
    half = xcf_s.shape[0]
    ln = 2 * half
    rb = rev_ref.shape[0]

    @pl.when(pl.program_id(1) == 0)
    def _():
        for b in range(half // rb):
            lo = ln - rb * b - rb

            def reversed_rows(ref):
                near = ref[lo:lo + rb, :]
                far = ref[lo + rb:lo + 2 * rb, :] if b > 0 else jnp.zeros_like(near)
                return _dot(rev_ref[...], jnp.concatenate([near, far], axis=0))

            rows = slice(rb * b, rb * (b + 1))
            xcf_s[rows, :] = (xc_ref[rows, :].astype(F32) + reversed_rows(xc_ref)).astype(BF16)
            xsf_s[rows, :] = (xs_ref[rows, :].astype(F32) - reversed_rows(xs_ref)).astype(BF16)

    odd = lax.broadcasted_iota(jnp.int32, (tr, 1), 0) & 1
    alt = jnp.where(odd == 1, -1.0, 1.0) * (1.0 / math.sqrt(2 * half))
    ya = (_dot(dc_ref[...], xcf_s[...]) + _dot(ds_ref[...], xsf_s[...])
          + alt * xc_ref[half:half + 1, :].astype(F32))
    h1 = (ya * ga_ref[...].astype(F32)).astype(BF16)
    out = _dot(h1, w_ref[0:d, :]) + _dot(yb_ref[...], w_ref[d:2 * d, :])
    o_ref[...] = _residual(x_ref[...], out, mod_ref, fg_ref)


def _dft_tables(n, dtype=BF16):
    nb = DFT_INNER if n % DFT_INNER == 0 else 1
    k = jnp.arange(n, dtype=jnp.int32)[:, None]

    def base(cols):
        ang = ((k * cols[None, :]) % n).astype(F32) * (2.0 * math.pi / n)
        return jnp.cos(ang), jnp.sin(ang)

    ca, sa = base(jnp.arange(n // nb, dtype=jnp.int32) * nb)
    cb, sb = base(jnp.arange(nb, dtype=jnp.int32))
    ca, sa, cb, sb = ca[:, :, None], sa[:, :, None], cb[:, None, :], sb[:, None, :]
    s = 1.0 / math.sqrt(n)
    cos = ((ca * cb - sa * sb) * s).reshape(n, n)
    sin = ((sa * cb + ca * sb) * s).reshape(n, n)
    return cos.astype(dtype), sin.astype(dtype)


def _token_tile(seq, t, cap):
    tile = min(cap, t if seq.shared_cond else seq.length)
    assert t % tile == 0 and (seq.length % tile == 0 or tile % seq.length == 0)
    return tile


def _fc_layer(x2d, seq, mod3, mod_row, norm_g, w_in, conv_w, w_out, li, row_len, final_g):
    t, d = x2d.shape
    tm = _token_tile(seq, t, FC_IN_TILE)
    assert tm % row_len == 0
    tiles_per_seq = max(seq.length // tm, 1)
    gd = d // FOURIER_GROUPS
    cc, sc = _dft_tables(gd)
    dft_ch = jnp.concatenate([cc, sc], axis=1)
    row_of_tile = lambda i: mod_row(i // tiles_per_seq)
    const = lambda *_: (0, 0)
    tok = pl.BlockSpec((tm, d), lambda i: (i, 0))
    xc, xs, ga, yb = pl.pallas_call(
        functools.partial(_fc_in_kernel, row_len=row_len, col_block=min(256, d)),
        grid=(t // tm,),
        in_specs=[
            tok,
            pl.BlockSpec((1, d), const),
            pl.BlockSpec((None, 1, 3 * d), lambda i: (row_of_tile(i), 0, 0)),
            pl.BlockSpec((None, d, 6 * d), lambda i: (li, 0, 0), pipeline_mode=pl.Buffered(1)),
            pl.BlockSpec((3, d), const),
            pl.BlockSpec((gd, 2 * gd), const),
        ],
        out_specs=[tok, tok, tok, tok],
        out_shape=[jax.ShapeDtypeStruct((t, d), BF16)] * 4,
        compiler_params=_params(1),
        name="fc_in",
    )(x2d, norm_g[None], mod3, w_in, conv_w, dft_ch)

    ln = seq.length
    tr = min(TOKEN_TILE, ln)
    rt = ln // tr
    cl, sl = _dft_tables(ln)
    final = final_g is not None
    row = pl.BlockSpec((tr, d), lambda s, r: (s * rt + r, 0))
    whole = pl.BlockSpec((ln, d), lambda s, r: (s, 0))
    half = ln // 2
    rb = min(REVERSE_BLOCK, half)
    assert ln % 2 == 0 and tr % 2 == 0 and half % rb == 0
    ri = jnp.arange(rb, dtype=jnp.int32)[:, None]
    rev = jnp.where(jnp.arange(2 * rb, dtype=jnp.int32)[None, :] == rb - ri, 1.0, 0.0).astype(BF16)
    in_specs = [
        pl.BlockSpec((tr, half), lambda s, r: (r, 0)),
        pl.BlockSpec((tr, half), lambda s, r: (r, 0)),
        pl.BlockSpec((rb, 2 * rb), lambda s, r: (0, 0)),
        whole, whole, row, row,
        pl.BlockSpec((None, 2 * d, d), lambda s, r: (li, 0, 0), pipeline_mode=pl.Buffered(1)),
        row,
        pl.BlockSpec((None, 1, 3 * d), lambda s, r: (mod_row(s), 0, 0)),
    ]
    args = [cl, -sl, rev, xc, xs, ga, yb, w_out, x2d, mod3]
    if final:
        in_specs.append(pl.BlockSpec((1, d), lambda s, r: (0, 0)))
        args.append(final_g[None])
    return pl.pallas_call(
        functools.partial(_fc_out_kernel, final=final),
        grid=(seq.count, rt),
        in_specs=in_specs,
        out_specs=row,
        out_shape=jax.ShapeDtypeStruct((t, d), F32),
        scratch_shapes=[pltpu.VMEM((half, d), BF16), pltpu.VMEM((half, d), BF16)],
        compiler_params=_params(2),
        name="fc_out",
    )(*args)


TAB_DT, TAB_CUM2, TAB_ROW2, TAB_COEF, TAB_E, TAB_CD = range(6)
LOG2_E = math.log2(math.e)
LOG2_FLOOR = -1e30


def _ssd_in_kernel(x_ref, g_ref, mod_ref, wt_ref, wdt_ref, dtb_ref, alog_ref, cw_ref,
                   proj_ref, tab_ref, hm_s, *, seq_len, gate_blocks, sub_rows):
    j = pl.program_id(1)
    nt = x_ref.shape[0]
    rows = wt_ref.shape[0]
    nc = nt // CHUNK

    def emit(r0, y):
        yb = y.astype(BF16)
        for k in range(nc):
            proj_ref[k, pl.ds(r0, sub_rows), :] = yb[:, k * CHUNK:(k + 1) * CHUNK]

    @pl.when(j == 0)
    def _():
        hm = _modnorm(x_ref[...], g_ref, mod_ref).astype(BF16)
        hm_s[...] = hm
        _scan_tables(hm, wdt_ref, dtb_ref, alog_ref, tab_ref)

    @pl.when(j < gate_blocks)
    def _():
        for r0 in range(0, rows, sub_rows):
            emit(r0, _silu(_dot_nt(wt_ref[r0:r0 + sub_rows, :], hm_s[...])))

    @pl.when(j >= gate_blocks)
    def _():
        lane0 = lax.broadcasted_iota(jnp.int32, (sub_rows, LANES), 1)

        @pl.loop(0, rows // sub_rows)
        def _(i):
            r0 = pl.multiple_of(i * sub_rows, sub_rows)
            p = _dot_nt(wt_ref[pl.ds(r0, sub_rows), :], hm_s[...])
            left = pltpu.roll(p, 1, 1)
            right = pltpu.roll(p, nt - 1, 1)
            lcols = [left[:, q:q + LANES] for q in range(0, nt, LANES)]
            rcols = [right[:, q:q + LANES] for q in range(0, nt, LANES)]
            for q in range(0, nt, seq_len):
                lcols[q // LANES] = jnp.where(lane0 == 0, 0.0, lcols[q // LANES])
                e = (q + seq_len) // LANES - 1
                rcols[e] = jnp.where(lane0 == LANES - 1, 0.0, rcols[e])
            left = jnp.concatenate(lcols, axis=1)
            right = jnp.concatenate(rcols, axis=1)
            cw = cw_ref[pl.ds(r0, sub_rows), :]
            y = left * cw[:, 0:1] + p * cw[:, 1:2] + right * cw[:, 2:3] + cw[:, 3:4]
            emit(r0, _silu(y))


def _scan_tables(hm, wdt_ref, dtb_ref, alog_ref, tab_ref):
    v = _dot_nt(wdt_ref[...], hm) + dtb_ref[...]
    dt_all = jnp.maximum(v, 0.0) + jnp.log(1.0 + jnp.exp(-jnp.abs(v)))
    heads2, nt = dt_all.shape
    nc = nt // CHUNK
    la_all = dt_all * -jnp.exp(alog_ref[...])
    dt = jnp.concatenate([dt_all[:, k * CHUNK:(k + 1) * CHUNK] for k in range(nc)], axis=0)
    la = jnp.concatenate([la_all[:, k * CHUNK:(k + 1) * CHUNK] for k in range(nc)], axis=0)
    p0 = la.astype(BF16)
    r1 = la - p0.astype(F32)
    p1 = r1.astype(BF16)
    p2 = (r1 - p1.astype(F32)).astype(BF16)
    ji = lax.broadcasted_iota(jnp.int32, (CHUNK, CHUNK), 0)
    li = lax.broadcasted_iota(jnp.int32, (CHUNK, CHUNK), 1)
    upto = jnp.where(ji <= li, 1.0, 0.0).astype(BF16)
    from_ = jnp.where(ji >= li, 1.0, 0.0).astype(BF16)
    pre = _dot(p0, upto) + _dot(p1, upto) + _dot(p2, upto)
    suf = _dot(p0, from_) + _dot(p1, from_) + _dot(p2, from_)
    row = lax.broadcasted_iota(jnp.int32, la.shape, 0)
    is_fwd = (row & (2 * HEADS_PER_GROUP - 1)) < HEADS_PER_GROUP
    cum = jnp.where(is_fwd, pre, suf)
    tot = jnp.where(is_fwd, jnp.broadcast_to(cum[:, CHUNK - 1:CHUNK], cum.shape),
                    jnp.broadcast_to(cum[:, 0:1], cum.shape))
    coef = dt * jnp.exp(tot - cum)
    e = jnp.exp(cum)
    cd = jnp.exp(tot)
    cum2 = cum * LOG2_E
    row2 = cum2 - jnp.maximum(jnp.log2(dt), LOG2_FLOOR)
    for k in range(nc):
        rows = slice(k * heads2, (k + 1) * heads2)
        tab_ref[k, TAB_DT] = dt[rows]
        tab_ref[k, TAB_CUM2] = cum2[rows]
        tab_ref[k, TAB_ROW2] = row2[rows]
        tab_ref[k, TAB_COEF] = coef[rows]
        tab_ref[k, TAB_E] = e[rows]
        tab_ref[k, TAB_CD] = cd[rows]


def _ssd_scan_kernel(sz_ref, x_ref, b_ref, c_ref, tab_ref, dcol_ref, gcol_ref,
                     *rest, seq_chunks, has_h0, has_prev, emit_state, unroll):
    rest = list(rest)
    h0_ref = rest.pop(0) if has_h0 else None
    prev_ref = rest.pop(0) if has_prev else None
    y_ref = rest.pop(0)
    st_ref = rest.pop(0) if emit_state else None
    yp_s, stf_s, stb_s, hf_s, hb_s, seg_s, cb_s, bm_s = rest

    nc, gw, _ = x_ref.shape
    n = b_ref.shape[1]
    hp = gw // HEADS_PER_GROUP
    nh = HEADS_PER_GROUP
    r2 = 2 * nh
    assert not has_h0 or nc == seq_chunks

    li = lax.broadcasted_iota(jnp.int32, (CHUNK, CHUNK), 0)
    si = lax.broadcasted_iota(jnp.int32, (CHUNK, CHUNK), 1)

    def expand(rows4):
        return jnp.concatenate(
            [jnp.broadcast_to(rows4[h:h + 1, :], (hp, LANES)) for h in range(nh)], axis=0)

    blk = (lax.broadcasted_iota(jnp.int32, (r2, r2 * CHUNK), 1) // CHUNK
           == lax.broadcasted_iota(jnp.int32, (r2, r2 * CHUNK), 0))
    sel_k = jnp.concatenate([jnp.where(blk, 1.0, 0.0)] * 3, axis=0)
    ones_k = jnp.ones((3 * r2, CHUNK), F32)

    def split3(v):
        p0 = v.astype(BF16).astype(F32)
        p1 = (v - p0).astype(BF16).astype(F32)
        return p0, p1, v - p0 - p1

    def seg_chunk(c, carry):
        lhs = jnp.concatenate([-p for p in split3(tab_ref[c, TAB_ROW2])] + [ones_k], axis=0)
        rhs = jnp.concatenate(
            [sel_k] + [jnp.where(blk, jnp.concatenate([p] * r2, axis=1), 0.0)
                       for p in split3(tab_ref[c, TAB_CUM2])], axis=0)
        seg_s[c] = lax.dot_general(lhs.astype(BF16), rhs.astype(BF16), (((0,), (0,)), ((), ())),
                                   preferred_element_type=F32)
        bm = b_ref[c].T
        bm_s[c] = bm
        cb_s[c] = _dot(bm, c_ref[c])
        return carry

    lax.fori_loop(0, nc, seg_chunk, 0, unroll=unroll)

    def local_chunk(c, carry):
        xst_b = x_ref[c]
        xst = xst_b.astype(F32)
        cb = cb_s[c]
        parts = []
        for h in range(nh):
            hb = nh + h
            arg = jnp.where(li <= si, seg_s[c, :, h * CHUNK:(h + 1) * CHUNK],
                            seg_s[c, :, hb * CHUNK:(hb + 1) * CHUNK])
            w = (cb * jnp.exp2(arg)).astype(BF16)
            parts.append(_dot(xst_b[h * hp:(h + 1) * hp, :], w))
        cb_diag = jnp.sum(jnp.where(si == li, cb, 0.0), axis=0, keepdims=True)
        skip = dcol_ref[...] + expand(tab_ref[c, TAB_DT][nh:r2] * cb_diag)
        yp_s[c] = jnp.concatenate(parts, axis=0) + skip * xst

        coef8 = tab_ref[c, TAB_COEF]
        xdw = jnp.concatenate([xst * expand(coef8[0:nh]), xst * expand(coef8[nh:r2])], axis=0)
        st = _dot(xdw.astype(BF16), bm_s[c])
        stf_s[c] = st[0:gw]
        stb_s[c] = st[gw:2 * gw]
        return carry

    lax.fori_loop(0, nc, local_chunk, 0, unroll=unroll)

    if has_h0:
        h0f = h0_ref[0].reshape(gw, n)
        h0b = h0_ref[1].reshape(gw, n)
    else:
        h0f = jnp.zeros((gw, n), F32)
        h0b = h0f
    if has_prev:
        st_ref[:, 0:prev_ref.shape[1]] = prev_ref[...]
    slot = st_ref.shape[1] - 1 if emit_state else None

    def fwd_state(c, h):
        hf_s[c] = h.astype(BF16)
        return h * expand(tab_ref[c, TAB_CD][0:nh]) + stf_s[c]

    def bwd_state(c, h):
        hb_s[c] = h.astype(BF16)
        return h * expand(tab_ref[c, TAB_CD][nh:r2]) + stb_s[c]

    for q in range(nc // seq_chunks):
        c0, c1 = q * seq_chunks, (q + 1) * seq_chunks
        hf = lax.fori_loop(c0, c1, fwd_state, h0f)
        hb = lax.fori_loop(0, seq_chunks, lambda i, h: bwd_state(c1 - 1 - i, h), h0b)
        if emit_state:
            st_ref[q, slot, 0] = hf.reshape(nh, hp, n)
            st_ref[q, slot, 1] = hb.reshape(nh, hp, n)

    def output_chunk(c, carry):
        hin = jnp.concatenate([hf_s[c], hb_s[c]], axis=0)
        yo = _dot(hin, c_ref[c])
        e8 = tab_ref[c, TAB_E]
        y_t = yp_s[c] + yo[0:gw] * expand(e8[0:nh]) + yo[gw:2 * gw] * expand(e8[nh:r2])
        y = y_t * sz_ref[c].astype(F32)
        inv = lax.rsqrt(jnp.mean(y * y, axis=0, keepdims=True) + EPS)
        y_ref[c] = (y * inv * gcol_ref[...]).astype(BF16)
        return carry

    lax.fori_loop(0, nc, output_chunk, 0, unroll=unroll)


def _ssd_out_kernel(y_ref, w_ref, x_ref, mod_ref, *rest, final):
    fg_ref = rest[0] if final else None
    o_ref = rest[-1]
    y = jnp.concatenate([y_ref[k].T for k in range(y_ref.shape[0])], axis=0)
    out = _dot(y, w_ref[...])
    o_ref[...] = _residual(x_ref[...], out, mod_ref, fg_ref)


def _ssd_layer(x2d, seq, mod3, mod_row, norm_g, w_zxbc, wdt_t, dt_bias_col, alog_col,
               conv_wb, d_col, gn_g, w_out, h0, prev_state, layer_idx, emit_state, final_g):
    t, d = x2d.shape
    d_inner = w_out.shape[1]
    heads2 = wdt_t.shape[1]
    groups = heads2 // (2 * HEADS_PER_GROUP)
    gw = d_inner // groups
    n = (w_zxbc.shape[1] - 2 * d_inner) // (2 * groups)
    hp = gw // HEADS_PER_GROUP
    ln = seq.length
    nc = ln // CHUNK
    tm = _token_tile(seq, t, SSD_OUT_TILE)
    tiles_per_seq = max(ln // tm, 1)
    row_of_tile = lambda i: mod_row(i // tiles_per_seq)
    const = lambda *_: (0, 0)
    tok = pl.BlockSpec((tm, d), lambda i: (i, 0))
    wrows = w_zxbc.shape[1]

    nt = ln * max(1, min(SSD_IN_TOKENS, t) // ln) if seq.shared_cond else ln
    seqs_per_block = nt // ln
    rows_blk = min(SSD_IN_ROWS, d_inner)
    assert t % nt == 0 and d_inner % rows_blk == 0 and wrows % rows_blk == 0
    n_tab = TAB_CD + 1
    gate_blocks = d_inner // rows_blk
    proj, tab = pl.pallas_call(
        functools.partial(_ssd_in_kernel, seq_len=ln, gate_blocks=gate_blocks,
                          sub_rows=min(SSD_IN_SUB_ROWS, rows_blk)),
        grid=(t // nt, wrows // rows_blk),
        in_specs=[
            pl.BlockSpec((nt, d), lambda i, j: (i, 0)),
            pl.BlockSpec((1, d), const),
            pl.BlockSpec((None, 1, 3 * d), lambda i, j: (mod_row(i * seqs_per_block), 0, 0)),
            pl.BlockSpec((None, rows_blk, d), lambda i, j: (layer_idx, j, 0)),
            pl.BlockSpec((None, heads2, d), lambda i, j: (layer_idx, 0, 0)),
            pl.BlockSpec((heads2, 1), const),
            pl.BlockSpec((heads2, 1), const),
            pl.BlockSpec((rows_blk, 4), lambda i, j: (jnp.maximum(j - gate_blocks, 0), 0)),
        ],
        out_specs=[
            pl.BlockSpec((nt // CHUNK, rows_blk, CHUNK), lambda i, j: (i, j, 0)),
            pl.BlockSpec((nt // CHUNK, n_tab, heads2, CHUNK), lambda i, j: (i, 0, 0, 0)),
        ],
        out_shape=[
            jax.ShapeDtypeStruct((t // CHUNK, wrows, CHUNK), BF16),
            jax.ShapeDtypeStruct((t // CHUNK, n_tab, heads2, CHUNK), F32),
        ],
        scratch_shapes=[pltpu.VMEM((nt, d), BF16)],
        compiler_params=_params(2),
        name="ssd_in",
    )(x2d, norm_g[None], mod3, w_zxbc, wdt_t, dt_bias_col, alog_col, conv_wb)

    r2 = 2 * HEADS_PER_GROUP
    xg0 = d_inner // gw
    bg0 = 2 * d_inner // n
    cg0 = bg0 + groups
    has_h0 = h0 is not None
    has_prev = emit_state and prev_state is not None
    spc = 1 if has_h0 else max(1, min(seq.count, SCAN_CELL_CHUNKS // nc))
    assert seq.count % spc == 0
    cc = spc * nc
    nh = HEADS_PER_GROUP
    in_specs = [
        pl.BlockSpec((cc, gw, CHUNK), lambda s, g: (s, g, 0)),
        pl.BlockSpec((cc, gw, CHUNK), lambda s, g: (s, xg0 + g, 0)),
        pl.BlockSpec((cc, n, CHUNK), lambda s, g: (s, bg0 + g, 0)),
        pl.BlockSpec((cc, n, CHUNK), lambda s, g: (s, cg0 + g, 0)),
        pl.BlockSpec((cc, n_tab, r2, CHUNK), lambda s, g: (s, 0, g, 0)),
        pl.BlockSpec((gw, LANES), lambda s, g: (g, 0)),
        pl.BlockSpec((gw, LANES), lambda s, g: (g, 0)),
    ]
    args = [proj, proj, proj, proj, tab, jnp.broadcast_to(d_col, (d_inner, LANES)),
            jnp.broadcast_to(gn_g[:, None], (d_inner, LANES))]
    if has_h0:
        in_specs.append(pl.BlockSpec((None, None, 2, nh, hp, n),
                                     lambda s, g: (s, layer_idx, 0, g, 0, 0)))
        args.append(h0)
    if has_prev:
        in_specs.append(pl.BlockSpec((spc, prev_state.shape[1], 2, nh, hp, n),
                                     lambda s, g: (s, 0, 0, g, 0, 0)))
        args.append(prev_state)
    out_specs = [pl.BlockSpec((cc, gw, CHUNK), lambda s, g: (s, g, 0))]
    out_shape = [jax.ShapeDtypeStruct((t // CHUNK, d_inner, CHUNK), BF16)]
    if emit_state:
        slots = (prev_state.shape[1] if has_prev else 0) + 1
        out_specs.append(pl.BlockSpec((spc, slots, 2, nh, hp, n), lambda s, g: (s, 0, 0, g, 0, 0)))
        out_shape.append(jax.ShapeDtypeStruct((seq.count, slots, 2, groups * nh, hp, n), F32))
    res = pl.pallas_call(
        functools.partial(_ssd_scan_kernel, seq_chunks=nc, has_h0=has_h0, has_prev=has_prev,
                          emit_state=emit_state, unroll=min(16, cc)),
        grid=(seq.count // spc, groups),
        in_specs=in_specs,
        out_specs=out_specs,
        out_shape=out_shape,
        scratch_shapes=[
            pltpu.VMEM((cc, gw, CHUNK), F32),
            pltpu.VMEM((cc, gw, n), F32),
            pltpu.VMEM((cc, gw, n), F32),
            pltpu.VMEM((cc, gw, n), BF16),
            pltpu.VMEM((cc, gw, n), BF16),
            pltpu.VMEM((cc, CHUNK, r2 * CHUNK), F32),
            pltpu.VMEM((cc, CHUNK, CHUNK), F32),
            pltpu.VMEM((cc, CHUNK, n), BF16),
        ],
        compiler_params=_params(2),
        name="ssd_scan",
    )(*args)
    y, state = (res[0], res[1]) if emit_state else (res[0], None)

    final = final_g is not None
    in_specs = [
        pl.BlockSpec((tm // CHUNK, d_inner, CHUNK), lambda i: (i, 0, 0)),
        pl.BlockSpec((None, d_inner, d), lambda i: (layer_idx, 0, 0), pipeline_mode=pl.Buffered(1)),
        tok,
        pl.BlockSpec((None, 1, 3 * d), lambda i: (row_of_tile(i), 0, 0)),
    ]
    args = [y, w_out, x2d, mod3]
    if final:
        in_specs.append(pl.BlockSpec((1, d), const))
        args.append(final_g[None])
    x_new = pl.pallas_call(
        functools.partial(_ssd_out_kernel, final=final),
        grid=(t // tm,),
        in_specs=in_specs,
        out_specs=tok,
        out_shape=jax.ShapeDtypeStruct((t, d), F32),
        compiler_params=_params(1),
        name="ssd_out",
    )(*args)
    return x_new, state


class _Seqs:
    def __init__(self, count, length, shared_cond):
        self.count = count
        self.length = length
        self.shared_cond = shared_cond


def _trunk(x, mod, mod_row, shared_cond, row_len, h0, emit_state, weights):
    (norm_g, fc_w_in, fc_conv_w, fc_w_out, ssd_w, ssd_wdt_t, ssd_dtb, ssd_alog, ssd_conv_wb,
     ssd_dcol, ssd_norm_g, ssd_w_out, final_norm_g) = weights
    bsz, ln, d = x.shape
    seq = _Seqs(bsz, ln, shared_cond)
    x2d = x.reshape(bsz * ln, d)
    depth = norm_g.shape[0]
    states = None
    for layer in range(depth):
        i = layer // 2
        mod3 = mod[layer].reshape(COND_ROWS, 1, 3 * d)
        final_g = final_norm_g if layer == depth - 1 else None
        if layer % 2 == 0:
            x2d = _fc_layer(x2d, seq, mod3, mod_row, norm_g[layer], fc_w_in, fc_conv_w[i],
                            fc_w_out, i, row_len, final_g)
        else:
            x2d, st = _ssd_layer(x2d, seq, mod3, mod_row, norm_g[layer], ssd_w, ssd_wdt_t,
                                 ssd_dtb[i], ssd_alog[i], ssd_conv_wb[i], ssd_dcol[i],
                                 ssd_norm_g[i], ssd_w_out, h0, states, i, emit_state, final_g)
            states = st
    return x2d.reshape(bsz, ln, d), states


def kernel(x_prompt, x_sample, state_ssm, c, c_ctx, w_mod, b_mod, norm_g, fc_w_in, fc_conv_w,
           fc_w_out, ssd_w_in, ssd_conv_w, ssd_conv_b, ssd_dt_bias, ssd_a_log, ssd_d,
           ssd_norm_g, ssd_w_out, final_norm_g):
    d = x_prompt.shape[-1]
    dec_batch = x_sample.shape[0]
    no, _, heads = ssd_a_log.shape
    d_inner = ssd_w_out.shape[1]
    conv_dim = ssd_conv_w.shape[-1]
    hp = d_inner // heads
    groups = heads // HEADS_PER_GROUP
    assert dec_batch + 1 <= COND_ROWS and heads % HEADS_PER_GROUP == 0

    cond = jnp.zeros((COND_ROWS, d), F32).at[:dec_batch].set(c).at[dec_batch].set(c_ctx)
    mod = _modulation(cond, w_mod, b_mod)

    zx = d_inner + conv_dim
    ssd_w = jnp.swapaxes(ssd_w_in[:, :, :zx], 1, 2).astype(BF16)
    ssd_conv_wb = jnp.swapaxes(jnp.concatenate([ssd_conv_w, ssd_conv_b[:, None]], axis=1), 1, 2)
    perm = jnp.arange(2 * heads).reshape(2, groups, HEADS_PER_GROUP).transpose(1, 0, 2).reshape(-1)
    ssd_wdt_t = jnp.swapaxes(ssd_w_in[:, :, zx:][:, :, perm], 1, 2).astype(BF16)
    ssd_dtb = ssd_dt_bias.reshape(no, 2 * heads)[:, perm][:, :, None]
    ssd_alog = ssd_a_log.reshape(no, 2 * heads)[:, perm][:, :, None]
    ssd_dcol = jnp.repeat(ssd_d, hp, axis=1)[:, :, None]
    weights = (norm_g, fc_w_in.astype(BF16), fc_conv_w, fc_w_out.astype(BF16), ssd_w, ssd_wdt_t,
               ssd_dtb, ssd_alog, ssd_conv_wb, ssd_dcol, ssd_norm_g,
               ssd_w_out.astype(BF16), final_norm_g)

    y_prompt, states = _trunk(x_prompt, mod, lambda s: dec_batch, True, x_prompt.shape[1], None,
                              True, weights)
    y_sample, _ = _trunk(x_sample, mod, lambda s: s, False, GRID_W, state_ssm, False, weights)
    return y_prompt, y_sample, states
```

```python
import functools
import math

import jax
import jax.numpy as jnp
from jax import lax
from jax.experimental import pallas as pl
from jax.experimental.pallas import tpu as pltpu

F32 = jnp.float32
BF16 = jnp.bfloat16

EPS = 1e-6
GRID_W = 64
FOURIER_GROUPS = 4
DFT_INNER = 128
REVERSE_BLOCK = 128
F32_SUBLANES = 8
CHUNK = 128
HEADS_PER_GROUP = 4
LANES = 128
TOKEN_TILE = 512
FC_IN_TILE = 1024
SSD_OUT_TILE = 1024
SSD_IN_TOKENS = 2048
SSD_IN_ROWS = 1024
SSD_IN_SUB_ROWS = 256
SCAN_CELL_CHUNKS = 16
COND_ROWS = 16
VMEM_LIMIT = 56 * 1024 * 1024


def _params(n_grid):
    return pltpu.CompilerParams(
        dimension_semantics=("arbitrary",) * n_grid, vmem_limit_bytes=VMEM_LIMIT)


def _dot(a, b):
    return jnp.dot(a, b, preferred_element_type=F32)


def _dot_nt(a, b):
    return lax.dot_general(a, b, (((1,), (1,)), ((), ())), preferred_element_type=F32)


def _sigmoid(x):
    return 1.0 / (1.0 + jnp.exp(-x))


def _silu(x):
    return x * _sigmoid(x)


def _rms(x):
    return x * lax.rsqrt(jnp.mean(x * x, axis=-1, keepdims=True) + EPS)


def _modnorm(x, g_ref, mod_ref):
    d = x.shape[-1]
    shift = mod_ref[:, 0:d]
    scale = mod_ref[:, d:2 * d]
    return _rms(x) * g_ref[...] * (1.0 + scale) + shift


def _residual(x, out, mod_ref, fg_ref):
    d = x.shape[-1]
    xn = x + mod_ref[:, 2 * d:3 * d] * out
    if fg_ref is not None:
        xn = _rms(xn) * fg_ref[...]
    return xn


def _mod_kernel(c_ref, w_ref, b_ref, o_ref):
    act = _silu(c_ref[...]).astype(BF16)
    o_ref[...] = _dot(act, w_ref[...].astype(BF16)) + b_ref[...]


def _modulation(cond, w_mod, b_mod):
    depth, d, d3 = w_mod.shape
    nb = d3 // d
    return pl.pallas_call(
        _mod_kernel,
        grid=(depth, nb),
        in_specs=[
            pl.BlockSpec((COND_ROWS, d), lambda l, j: (0, 0)),
            pl.BlockSpec((None, d, d), lambda l, j: (l, 0, j)),
            pl.BlockSpec((None, 1, d), lambda l, j: (l, 0, j)),
        ],
        out_specs=pl.BlockSpec((None, COND_ROWS, d), lambda l, j: (l, 0, j)),
        out_shape=jax.ShapeDtypeStruct((depth, COND_ROWS, d3), F32),
        compiler_params=_params(2),
        name="modulation",
    )(cond, w_mod, b_mod.reshape(depth, 1, d3))


def _fc_in_kernel(x_ref, g_ref, mod_ref, w_ref, cw_ref, dft_ref,
                  xc_ref, xs_ref, ga_ref, yb_ref, *, row_len, col_block):
    tm, d = x_ref.shape
    gd = d // FOURIER_GROUPS
    hm = _modnorm(x_ref[...], g_ref, mod_ref).astype(BF16)
    ua = _dot(hm, w_ref[:, 0:d]).astype(BF16)
    for g in range(FOURIER_GROUPS):
        t = _dot(ua[:, g * gd:(g + 1) * gd], dft_ref[...])
        xc_ref[:, g * gd:(g + 1) * gd] = t[:, :gd].astype(BF16)
        xs_ref[:, g * gd:(g + 1) * gd] = t[:, gd:].astype(BF16)
    pos = lax.rem(lax.broadcasted_iota(jnp.int32, (tm, 1), 0), row_len)
    first = pos == 0
    last = pos == row_len - 1
    for j in range(d // col_block):
        lo = j * col_block
        za = _dot(hm, w_ref[:, d + lo:d + lo + col_block])
        ga_ref[:, lo:lo + col_block] = _silu(za).astype(BF16)
        bb = _dot(hm, w_ref[:, 2 * d + lo:2 * d + lo + col_block])
        cc = _dot(hm, w_ref[:, 3 * d + lo:3 * d + lo + col_block])
        vv = _dot(hm, w_ref[:, 4 * d + lo:4 * d + lo + col_block])
        zb = _dot(hm, w_ref[:, 5 * d + lo:5 * d + lo + col_block])
        u = cc * vv
        up = jnp.where(first, 0.0, pltpu.roll(u, 1, 0))
        un = jnp.where(last, 0.0, pltpu.roll(u, tm - 1, 0))
        cw = cw_ref[:, lo:lo + col_block]
        y = up * cw[0:1] + u * cw[1:2] + un * cw[2:3]
        yb_ref[:, lo:lo + col_block] = (bb * y * _silu(zb)).astype(BF16)


def _fc_out_kernel(dc_ref, ds_ref, dch_ref, rev_ref, xc_ref, xs_ref, ga_ref, yb_ref, w_ref, x_ref,
                   mod_ref, *rest, final, direct_tiles, mirror):
    fg_ref = rest[0] if final else None
    o_ref, xcf_s, xsf_s, d_s = rest[-4:]
    r = pl.program_id(1)
    tr, d = x_ref.shape
    half = xcf_s.shape[0]
    ln = 2 * half
    rb = rev_ref.shape[0]

    @pl.when(pl.program_id(1) == 0)
    def _():
        for b in range(half // rb):
            lo = ln - rb * b - rb

            def reversed_rows(ref):
                near = ref[lo:lo + rb, :]
                far = ref[lo + rb:lo + 2 * rb, :] if b > 0 else jnp.zeros_like(near)
                return _dot(rev_ref[...], jnp.concatenate([near, far], axis=0))

            rows = slice(rb * b, rb * (b + 1))
            xcf_s[rows, :] = (xc_ref[rows, :].astype(F32) + reversed_rows(xc_ref)).astype(BF16)
            xsf_s[rows, :] = (xs_ref[rows, :].astype(F32) - reversed_rows(xs_ref)).astype(BF16)
        if mirror:
            y_half = (_dot(dch_ref[...], xcf_s[...])[0:1]
                      + xc_ref[half:half + 1, :].astype(F32) * (1.0 / math.sqrt(ln)))
            first = lax.broadcasted_iota(jnp.int32, (rb, 1), 0) == 0
            d_s[half:half + rb, :] = jnp.where(first, y_half, 0.0).astype(BF16)

    def finish(ya):
        h1 = (ya * ga_ref[...].astype(F32)).astype(BF16)
        out = _dot(h1, w_ref[0:d, :]) + _dot(yb_ref[...], w_ref[d:2 * d, :])
        o_ref[...] = _residual(x_ref[...], out, mod_ref, fg_ref)

    def direct():
        odd = lax.broadcasted_iota(jnp.int32, (tr, 1), 0) & 1
        alt = jnp.where(odd == 1, -1.0, 1.0) * (1.0 / math.sqrt(ln))
        a = _dot(dc_ref[...], xcf_s[...]) + alt * xc_ref[half:half + 1, :].astype(F32)
        b = _dot(ds_ref[...], xsf_s[...])
        if mirror:
            d_s[pl.ds(pl.multiple_of(r * tr, tr), tr), :] = (a - b).astype(BF16)
        finish(a + b)

    if not mirror:
        direct()
    else:
        pl.when(r < direct_tiles)(direct)

        @pl.when(r >= direct_tiles)
        def _():
            m0 = (r - direct_tiles) * tr
            parts = []
            for q in range(tr // rb):
                lo = pl.multiple_of(half - m0 - (q + 1) * rb, rb)
                parts.append(_dot(rev_ref[...], d_s[pl.ds(lo, 2 * rb), :]))
            finish(jnp.concatenate(parts, axis=0))


def _dft_tables(n, dtype=BF16):
    nb = DFT_INNER if n % DFT_INNER == 0 else 1
    k = jnp.arange(n, dtype=jnp.int32)[:, None]

    def base(cols):
        ang = ((k * cols[None, :]) % n).astype(F32) * (2.0 * math.pi / n)
        return jnp.cos(ang), jnp.sin(ang)

    ca, sa = base(jnp.arange(n // nb, dtype=jnp.int32) * nb)
    cb, sb = base(jnp.arange(nb, dtype=jnp.int32))
    ca, sa, cb, sb = ca[:, :, None], sa[:, :, None], cb[:, None, :], sb[:, None, :]
    s = 1.0 / math.sqrt(n)
    cos = ((ca * cb - sa * sb) * s).reshape(n, n)
    sin = ((sa * cb + ca * sb) * s).reshape(n, n)
    return cos.astype(dtype), sin.astype(dtype)


def _token_tile(seq, t, cap):
    tile = min(cap, t if seq.shared_cond else seq.length)
    assert t % tile == 0 and (seq.length % tile == 0 or tile % seq.length == 0)
    return tile


def _fc_layer(x2d, seq, mod3, mod_row, norm_g, w_in, conv_w, w_out, li, row_len, final_g):
    t, d = x2d.shape
    tm = _token_tile(seq, t, FC_IN_TILE)
    assert tm % row_len == 0
    tiles_per_seq = max(seq.length // tm, 1)
    gd = d // FOURIER_GROUPS
    cc, sc = _dft_tables(gd)
    dft_ch = jnp.concatenate([cc, sc], axis=1)
    row_of_tile = lambda i: mod_row(i // tiles_per_seq)
    const = lambda *_: (0, 0)
    tok = pl.BlockSpec((tm, d), lambda i: (i, 0))
    xc, xs, ga, yb = pl.pallas_call(
        functools.partial(_fc_in_kernel, row_len=row_len, col_block=min(256, d)),
        grid=(t // tm,),
        in_specs=[
            tok,
            pl.BlockSpec((1, d), const),
            pl.BlockSpec((None, 1, 3 * d), lambda i: (row_of_tile(i), 0, 0)),
            pl.BlockSpec((None, d, 6 * d), lambda i: (li, 0, 0), pipeline_mode=pl.Buffered(1)),
            pl.BlockSpec((3, d), const),
            pl.BlockSpec((gd, 2 * gd), const),
        ],
        out_specs=[tok, tok, tok, tok],
        out_shape=[jax.ShapeDtypeStruct((t, d), BF16)] * 4,
        compiler_params=_params(1),
        name="fc_in",
    )(x2d, norm_g[None], mod3, w_in, conv_w, dft_ch)

    ln = seq.length
    tr = min(TOKEN_TILE, ln)
    rt = ln // tr
    cl, sl = _dft_tables(ln)
    final = final_g is not None
    row = pl.BlockSpec((tr, d), lambda s, r: (s * rt + r, 0))
    whole = pl.BlockSpec((ln, d), lambda s, r: (s, 0))
    half = ln // 2
    rb = min(REVERSE_BLOCK, half)
    assert ln % 2 == 0 and tr % 2 == 0 and half % rb == 0
    ri = jnp.arange(rb, dtype=jnp.int32)[:, None]
    rev = jnp.where(jnp.arange(2 * rb, dtype=jnp.int32)[None, :] == rb - ri, 1.0, 0.0).astype(BF16)
    mirror = rt % 2 == 0
    direct_tiles = rt // 2 if mirror else rt
    table = pl.BlockSpec((tr, half), lambda s, r: (jnp.minimum(r, direct_tiles - 1), 0))
    in_specs = [
        table, table,
        pl.BlockSpec((F32_SUBLANES, half), lambda s, r: (half // F32_SUBLANES, 0)),
        pl.BlockSpec((rb, 2 * rb), lambda s, r: (0, 0)),
        whole, whole, row, row,
        pl.BlockSpec((None, 2 * d, d), lambda s, r: (li, 0, 0), pipeline_mode=pl.Buffered(1)),
        row,
        pl.BlockSpec((None, 1, 3 * d), lambda s, r: (mod_row(s), 0, 0)),
    ]
    args = [cl, -sl, cl, rev, xc, xs, ga, yb, w_out, x2d, mod3]
    if final:
        in_specs.append(pl.BlockSpec((1, d), lambda s, r: (0, 0)))
        args.append(final_g[None])
    return pl.pallas_call(
        functools.partial(_fc_out_kernel, final=final, direct_tiles=direct_tiles, mirror=mirror),
        grid=(seq.count, rt),
        in_specs=in_specs,
        out_specs=row,
        out_shape=jax.ShapeDtypeStruct((t, d), F32),
        scratch_shapes=[pltpu.VMEM((half, d), BF16), pltpu.VMEM((half, d), BF16),
                        pltpu.VMEM((half + rb, d), BF16)],
        compiler_params=_params(2),
        name="fc_out",
    )(*args)


TAB_DT, TAB_CUM2, TAB_ROW2, TAB_COEF, TAB_E, TAB_CD = range(6)
LOG2_E = math.log2(math.e)
LOG2_FLOOR = -1e30


def _ssd_in_kernel(x_ref, g_ref, mod_ref, wt_ref, wdt_ref, dtb_ref, alog_ref, cw_ref,
                   proj_ref, tab_ref, hm_s, *, seq_len, gate_blocks, sub_rows):
    j = pl.program_id(1)
    nt = x_ref.shape[0]
    rows = wt_ref.shape[0]
    nc = nt // CHUNK

    def emit(r0, y):
        yb = y.astype(BF16)
        for k in range(nc):
            proj_ref[k, pl.ds(r0, sub_rows), :] = yb[:, k * CHUNK:(k + 1) * CHUNK]

    @pl.when(j == 0)
    def _():
        hm = _modnorm(x_ref[...], g_ref, mod_ref).astype(BF16)
        hm_s[...] = hm
        _scan_tables(hm, wdt_ref, dtb_ref, alog_ref, tab_ref)

    @pl.when(j < gate_blocks)
    def _():
        for r0 in range(0, rows, sub_rows):
            emit(r0, _silu(_dot_nt(wt_ref[r0:r0 + sub_rows, :], hm_s[...])))

    @pl.when(j >= gate_blocks)
    def _():
        lane0 = lax.broadcasted_iota(jnp.int32, (sub_rows, LANES), 1)

        @pl.loop(0, rows // sub_rows)
        def _(i):
            r0 = pl.multiple_of(i * sub_rows, sub_rows)
            p = _dot_nt(wt_ref[pl.ds(r0, sub_rows), :], hm_s[...])
            left = pltpu.roll(p, 1, 1)
            right = pltpu.roll(p, nt - 1, 1)
            lcols = [left[:, q:q + LANES] for q in range(0, nt, LANES)]
            rcols = [right[:, q:q + LANES] for q in range(0, nt, LANES)]
            for q in range(0, nt, seq_len):
                lcols[q // LANES] = jnp.where(lane0 == 0, 0.0, lcols[q // LANES])
                e = (q + seq_len) // LANES - 1
                rcols[e] = jnp.where(lane0 == LANES - 1, 0.0, rcols[e])
            left = jnp.concatenate(lcols, axis=1)
            right = jnp.concatenate(rcols, axis=1)
            cw = cw_ref[pl.ds(r0, sub_rows), :]
            y = left * cw[:, 0:1] + p * cw[:, 1:2] + right * cw[:, 2:3] + cw[:, 3:4]
            emit(r0, _silu(y))


def _scan_tables(hm, wdt_ref, dtb_ref, alog_ref, tab_ref):
    v = _dot_nt(wdt_ref[...], hm) + dtb_ref[...]
    dt_all = jnp.maximum(v, 0.0) + jnp.log(1.0 + jnp.exp(-jnp.abs(v)))
    heads2, nt = dt_all.shape
    nc = nt // CHUNK
    la_all = dt_all * -jnp.exp(alog_ref[...])
    dt = jnp.concatenate([dt_all[:, k * CHUNK:(k + 1) * CHUNK] for k in range(nc)], axis=0)
    la = jnp.concatenate([la_all[:, k * CHUNK:(k + 1) * CHUNK] for k in range(nc)], axis=0)
    p0 = la.astype(BF16)
    r1 = la - p0.astype(F32)
    p1 = r1.astype(BF16)
    p2 = (r1 - p1.astype(F32)).astype(BF16)
    ji = lax.broadcasted_iota(jnp.int32, (CHUNK, CHUNK), 0)
    li = lax.broadcasted_iota(jnp.int32, (CHUNK, CHUNK), 1)
    upto = jnp.where(ji <= li, 1.0, 0.0).astype(BF16)
    from_ = jnp.where(ji >= li, 1.0, 0.0).astype(BF16)
    pre = _dot(p0, upto) + _dot(p1, upto) + _dot(p2, upto)
    suf = _dot(p0, from_) + _dot(p1, from_) + _dot(p2, from_)
    row = lax.broadcasted_iota(jnp.int32, la.shape, 0)
    is_fwd = (row & (2 * HEADS_PER_GROUP - 1)) < HEADS_PER_GROUP
    cum = jnp.where(is_fwd, pre, suf)
    tot = jnp.where(is_fwd, jnp.broadcast_to(cum[:, CHUNK - 1:CHUNK], cum.shape),
                    jnp.broadcast_to(cum[:, 0:1], cum.shape))
    coef = dt * jnp.exp(tot - cum)
    e = jnp.exp(cum)
    cd = jnp.exp(tot)
    cum2 = cum * LOG2_E
    row2 = cum2 - jnp.maximum(jnp.log2(dt), LOG2_FLOOR)
    for k in range(nc):
        rows = slice(k * heads2, (k + 1) * heads2)
        tab_ref[k, TAB_DT] = dt[rows]
        tab_ref[k, TAB_CUM2] = cum2[rows]
        tab_ref[k, TAB_ROW2] = row2[rows]
        tab_ref[k, TAB_COEF] = coef[rows]
        tab_ref[k, TAB_E] = e[rows]
        tab_ref[k, TAB_CD] = cd[rows]


def _ssd_scan_kernel(sz_ref, x_ref, b_ref, c_ref, tab_ref, dcol_ref, gcol_ref,
                     *rest, seq_chunks, has_h0, has_prev, emit_state, unroll):
    rest = list(rest)
    h0_ref = rest.pop(0) if has_h0 else None
    prev_ref = rest.pop(0) if has_prev else None
    y_ref = rest.pop(0)
    st_ref = rest.pop(0) if emit_state else None
    yp_s, stf_s, stb_s, hf_s, hb_s, seg_s, cb_s, bm_s = rest

    nc, gw, _ = x_ref.shape
    n = b_ref.shape[1]
    hp = gw // HEADS_PER_GROUP
    nh = HEADS_PER_GROUP
    r2 = 2 * nh
    assert not has_h0 or nc == seq_chunks

    li = lax.broadcasted_iota(jnp.int32, (CHUNK, CHUNK), 0)
    si = lax.broadcasted_iota(jnp.int32, (CHUNK, CHUNK), 1)

    def expand(rows4):
        return jnp.concatenate(
            [jnp.broadcast_to(rows4[h:h + 1, :], (hp, LANES)) for h in range(nh)], axis=0)

    blk = (lax.broadcasted_iota(jnp.int32, (r2, r2 * CHUNK), 1) // CHUNK
           == lax.broadcasted_iota(jnp.int32, (r2, r2 * CHUNK), 0))
    sel_k = jnp.concatenate([jnp.where(blk, 1.0, 0.0)] * 3, axis=0)
    ones_k = jnp.ones((3 * r2, CHUNK), F32)

    def split3(v):
        p0 = v.astype(BF16).astype(F32)
        p1 = (v - p0).astype(BF16).astype(F32)
        return p0, p1, v - p0 - p1

    def seg_chunk(c, carry):
        lhs = jnp.concatenate([-p for p in split3(tab_ref[c, TAB_ROW2])] + [ones_k], axis=0)
        rhs = jnp.concatenate(
            [sel_k] + [jnp.where(blk, jnp.concatenate([p] * r2, axis=1), 0.0)
                       for p in split3(tab_ref[c, TAB_CUM2])], axis=0)
        seg_s[c] = lax.dot_general(lhs.astype(BF16), rhs.astype(BF16), (((0,), (0,)), ((), ())),
                                   preferred_element_type=F32)
        bm = b_ref[c].T
        bm_s[c] = bm
        cb_s[c] = _dot(bm, c_ref[c])
        return carry

    lax.fori_loop(0, nc, seg_chunk, 0, unroll=unroll)

    def local_chunk(c, carry):
        xst_b = x_ref[c]
        xst = xst_b.astype(F32)
        cb = cb_s[c]
        parts = []
        for h in range(nh):
            hb = nh + h
            arg = jnp.where(li <= si, seg_s[c, :, h * CHUNK:(h + 1) * CHUNK],
                            seg_s[c, :, hb * CHUNK:(hb + 1) * CHUNK])
            w = (cb * jnp.exp2(arg)).astype(BF16)
            parts.append(_dot(xst_b[h * hp:(h + 1) * hp, :], w))
        cb_diag = jnp.sum(jnp.where(si == li, cb, 0.0), axis=0, keepdims=True)
        skip = dcol_ref[...] + expand(tab_ref[c, TAB_DT][nh:r2] * cb_diag)
        yp_s[c] = jnp.concatenate(parts, axis=0) + skip * xst

        coef8 = tab_ref[c, TAB_COEF]
        xdw = jnp.concatenate([xst * expand(coef8[0:nh]), xst * expand(coef8[nh:r2])], axis=0)
        st = _dot(xdw.astype(BF16), bm_s[c])
        stf_s[c] = st[0:gw]
        stb_s[c] = st[gw:2 * gw]
        return carry

    lax.fori_loop(0, nc, local_chunk, 0, unroll=unroll)

    if has_h0:
        h0f = h0_ref[0].reshape(gw, n)
        h0b = h0_ref[1].reshape(gw, n)
    else:
        h0f = jnp.zeros((gw, n), F32)
        h0b = h0f
    if has_prev:
        st_ref[:, 0:prev_ref.shape[1]] = prev_ref[...]
    slot = st_ref.shape[1] - 1 if emit_state else None

    def fwd_state(c, h):
        hf_s[c] = h.astype(BF16)
        return h * expand(tab_ref[c, TAB_CD][0:nh]) + stf_s[c]

    def bwd_state(c, h):
        hb_s[c] = h.astype(BF16)
        return h * expand(tab_ref[c, TAB_CD][nh:r2]) + stb_s[c]

    for q in range(nc // seq_chunks):
        c0, c1 = q * seq_chunks, (q + 1) * seq_chunks
        hf = lax.fori_loop(c0, c1, fwd_state, h0f)
        hb = lax.fori_loop(0, seq_chunks, lambda i, h: bwd_state(c1 - 1 - i, h), h0b)
        if emit_state:
            st_ref[q, slot, 0] = hf.reshape(nh, hp, n)
            st_ref[q, slot, 1] = hb.reshape(nh, hp, n)

    def output_chunk(c, carry):
        hin = jnp.concatenate([hf_s[c], hb_s[c]], axis=0)
        yo = _dot(hin, c_ref[c])
        e8 = tab_ref[c, TAB_E]
        y_t = yp_s[c] + yo[0:gw] * expand(e8[0:nh]) + yo[gw:2 * gw] * expand(e8[nh:r2])
        y = y_t * sz_ref[c].astype(F32)
        inv = lax.rsqrt(jnp.mean(y * y, axis=0, keepdims=True) + EPS)
        y_ref[c] = (y * inv * gcol_ref[...]).astype(BF16)
        return carry

    lax.fori_loop(0, nc, output_chunk, 0, unroll=unroll)


def _ssd_out_kernel(y_ref, w_ref, x_ref, mod_ref, *rest, final):
    fg_ref = rest[0] if final else None
    o_ref = rest[-1]
    y = jnp.concatenate([y_ref[k].T for k in range(y_ref.shape[0])], axis=0)
    out = _dot(y, w_ref[...])
    o_ref[...] = _residual(x_ref[...], out, mod_ref, fg_ref)


def _ssd_layer(x2d, seq, mod3, mod_row, norm_g, w_zxbc, wdt_t, dt_bias_col, alog_col,
               conv_wb, d_col, gn_g, w_out, h0, prev_state, layer_idx, emit_state, final_g):
    t, d = x2d.shape
    d_inner = w_out.shape[1]
    heads2 = wdt_t.shape[1]
    groups = heads2 // (2 * HEADS_PER_GROUP)
    gw = d_inner // groups
    n = (w_zxbc.shape[1] - 2 * d_inner) // (2 * groups)
    hp = gw // HEADS_PER_GROUP
    ln = seq.length
    nc = ln // CHUNK
    tm = _token_tile(seq, t, SSD_OUT_TILE)
    tiles_per_seq = max(ln // tm, 1)
    row_of_tile = lambda i: mod_row(i // tiles_per_seq)
    const = lambda *_: (0, 0)
    tok = pl.BlockSpec((tm, d), lambda i: (i, 0))
    wrows = w_zxbc.shape[1]

    nt = ln * max(1, min(SSD_IN_TOKENS, t) // ln) if seq.shared_cond else ln
    seqs_per_block = nt // ln
    rows_blk = min(SSD_IN_ROWS, d_inner)
    assert t % nt == 0 and d_inner % rows_blk == 0 and wrows % rows_blk == 0
    n_tab = TAB_CD + 1
    gate_blocks = d_inner // rows_blk
    proj, tab = pl.pallas_call(
        functools.partial(_ssd_in_kernel, seq_len=ln, gate_blocks=gate_blocks,
                          sub_rows=min(SSD_IN_SUB_ROWS, rows_blk)),
        grid=(t // nt, wrows // rows_blk),
        in_specs=[
            pl.BlockSpec((nt, d), lambda i, j: (i, 0)),
            pl.BlockSpec((1, d), const),
            pl.BlockSpec((None, 1, 3 * d), lambda i, j: (mod_row(i * seqs_per_block), 0, 0)),
            pl.BlockSpec((None, rows_blk, d), lambda i, j: (layer_idx, j, 0)),
            pl.BlockSpec((None, heads2, d), lambda i, j: (layer_idx, 0, 0)),
            pl.BlockSpec((heads2, 1), const),
            pl.BlockSpec((heads2, 1), const),
            pl.BlockSpec((rows_blk, 4), lambda i, j: (jnp.maximum(j - gate_blocks, 0), 0)),
        ],
        out_specs=[
            pl.BlockSpec((nt // CHUNK, rows_blk, CHUNK), lambda i, j: (i, j, 0)),
            pl.BlockSpec((nt // CHUNK, n_tab, heads2, CHUNK), lambda i, j: (i, 0, 0, 0)),
        ],
        out_shape=[
            jax.ShapeDtypeStruct((t // CHUNK, wrows, CHUNK), BF16),
            jax.ShapeDtypeStruct((t // CHUNK, n_tab, heads2, CHUNK), F32),
        ],
        scratch_shapes=[pltpu.VMEM((nt, d), BF16)],
        compiler_params=_params(2),
        name="ssd_in",
    )(x2d, norm_g[None], mod3, w_zxbc, wdt_t, dt_bias_col, alog_col, conv_wb)

    r2 = 2 * HEADS_PER_GROUP
    xg0 = d_inner // gw
    bg0 = 2 * d_inner // n
    cg0 = bg0 + groups
    has_h0 = h0 is not None
    has_prev = emit_state and prev_state is not None
    spc = 1 if has_h0 else max(1, min(seq.count, SCAN_CELL_CHUNKS // nc))
    assert seq.count % spc == 0
    cc = spc * nc
    nh = HEADS_PER_GROUP
    in_specs = [
        pl.BlockSpec((cc, gw, CHUNK), lambda s, g: (s, g, 0)),
        pl.BlockSpec((cc, gw, CHUNK), lambda s, g: (s, xg0 + g, 0)),
        pl.BlockSpec((cc, n, CHUNK), lambda s, g: (s, bg0 + g, 0)),
        pl.BlockSpec((cc, n, CHUNK), lambda s, g: (s, cg0 + g, 0)),
        pl.BlockSpec((cc, n_tab, r2, CHUNK), lambda s, g: (s, 0, g, 0)),
        pl.BlockSpec((gw, LANES), lambda s, g: (g, 0)),
        pl.BlockSpec((gw, LANES), lambda s, g: (g, 0)),
    ]
    args = [proj, proj, proj, proj, tab, jnp.broadcast_to(d_col, (d_inner, LANES)),
            jnp.broadcast_to(gn_g[:, None], (d_inner, LANES))]
    if has_h0:
        in_specs.append(pl.BlockSpec((None, None, 2, nh, hp, n),
                                     lambda s, g: (s, layer_idx, 0, g, 0, 0)))
        args.append(h0)
    if has_prev:
        in_specs.append(pl.BlockSpec((spc, prev_state.shape[1], 2, nh, hp, n),
                                     lambda s, g: (s, 0, 0, g, 0, 0)))
        args.append(prev_state)
    out_specs = [pl.BlockSpec((cc, gw, CHUNK), lambda s, g: (s, g, 0))]
    out_shape = [jax.ShapeDtypeStruct((t // CHUNK, d_inner, CHUNK), BF16)]
    if emit_state:
        slots = (prev_state.shape[1] if has_prev else 0) + 1
        out_specs.append(pl.BlockSpec((spc, slots, 2, nh, hp, n), lambda s, g: (s, 0, 0, g, 0, 0)))
        out_shape.append(jax.ShapeDtypeStruct((seq.count, slots, 2, groups * nh, hp, n), F32))
    res = pl.pallas_call(
        functools.partial(_ssd_scan_kernel, seq_chunks=nc, has_h0=has_h0, has_prev=has_prev,
                          emit_state=emit_state, unroll=min(16, cc)),
        grid=(seq.count // spc, groups),
        in_specs=in_specs,
        out_specs=out_specs,
        out_shape=out_shape,
        scratch_shapes=[
            pltpu.VMEM((cc, gw, CHUNK), F32),
            pltpu.VMEM((cc, gw, n), F32),
            pltpu.VMEM((cc, gw, n), F32),
            pltpu.VMEM((cc, gw, n), BF16),
            pltpu.VMEM((cc, gw, n), BF16),
            pltpu.VMEM((cc, CHUNK, r2 * CHUNK), F32),
            pltpu.VMEM((cc, CHUNK, CHUNK), F32),
            pltpu.VMEM((cc, CHUNK, n), BF16),
        ],
        compiler_params=_params(2),
        name="ssd_scan",
    )(*args)
    y, state = (res[0], res[1]) if emit_state else (res[0], None)

    final = final_g is not None
    in_specs = [
        pl.BlockSpec((tm // CHUNK, d_inner, CHUNK), lambda i: (i, 0, 0)),
        pl.BlockSpec((None, d_inner, d), lambda i: (layer_idx, 0, 0), pipeline_mode=pl.Buffered(1)),
        tok,
        pl.BlockSpec((None, 1, 3 * d), lambda i: (row_of_tile(i), 0, 0)),
    ]
    args = [y, w_out, x2d, mod3]
    if final:
        in_specs.append(pl.BlockSpec((1, d), const))
        args.append(final_g[None])
    x_new = pl.pallas_call(
        functools.partial(_ssd_out_kernel, final=final),
        grid=(t // tm,),
        in_specs=in_specs,
        out_specs=tok,
        out_shape=jax.ShapeDtypeStruct((t, d), F32),
        compiler_params=_params(1),
        name="ssd_out",
    )(*args)
    return x_new, state


class _Seqs:
    def __init__(self, count, length, shared_cond):
        self.count = count
        self.length = length
        self.shared_cond = shared_cond


def _trunk(x, mod, mod_row, shared_cond, row_len, h0, emit_state, weights):
    (norm_g, fc_w_in, fc_conv_w, fc_w_out, ssd_w, ssd_wdt_t, ssd_dtb, ssd_alog, ssd_conv_wb,
     ssd_dcol, ssd_norm_g, ssd_w_out, final_norm_g) = weights
    bsz, ln, d = x.shape
    seq = _Seqs(bsz, ln, shared_cond)
    x2d = x.reshape(bsz * ln, d)
    depth = norm_g.shape[0]
    states = None
    for layer in range(depth):
        i = layer // 2
        mod3 = mod[layer].reshape(COND_ROWS, 1, 3 * d)
        final_g = final_norm_g if layer == depth - 1 else None
        if layer % 2 == 0:
            x2d = _fc_layer(x2d, seq, mod3, mod_row, norm_g[layer], fc_w_in, fc_conv_w[i],
                            fc_w_out, i, row_len, final_g)
        else:
            x2d, st = _ssd_layer(x2d, seq, mod3, mod_row, norm_g[layer], ssd_w, ssd_wdt_t,
                                 ssd_dtb[i], ssd_alog[i], ssd_conv_wb[i], ssd_dcol[i],
                                 ssd_norm_g[i], ssd_w_out, h0, states, i, emit_state, final_g)
            states = st
    return x2d.reshape(bsz, ln, d), states


def kernel(x_prompt, x_sample, state_ssm, c, c_ctx, w_mod, b_mod, norm_g, fc_w_in, fc_conv_w,
           fc_w_out, ssd_w_in, ssd_conv_w, ssd_conv_b, ssd_dt_bias, ssd_a_log, ssd_d,
           ssd_norm_g, ssd_w_out, final_norm_g):
    d = x_prompt.shape[-1]
    dec_batch = x_sample.shape[0]
    no, _, heads = ssd_a_log.shape
    d_inner = ssd_w_out.shape[1]
    conv_dim = ssd_conv_w.shape[-1]
    hp = d_inner // heads
    groups = heads // HEADS_PER_GROUP
    assert dec_batch + 1 <= COND_ROWS and heads % HEADS_PER_GROUP == 0

    cond = jnp.zeros((COND_ROWS, d), F32).at[:dec_batch].set(c).at[dec_batch].set(c_ctx)
    mod = _modulation(cond, w_mod, b_mod)

    zx = d_inner + conv_dim
    ssd_w = jnp.swapaxes(ssd_w_in[:, :, :zx], 1, 2).astype(BF16)
    ssd_conv_wb = jnp.swapaxes(jnp.concatenate([ssd_conv_w, ssd_conv_b[:, None]], axis=1), 1, 2)
    perm = jnp.arange(2 * heads).reshape(2, groups, HEADS_PER_GROUP).transpose(1, 0, 2).reshape(-1)
    ssd_wdt_t = jnp.swapaxes(ssd_w_in[:, :, zx:][:, :, perm], 1, 2).astype(BF16)
    ssd_dtb = ssd_dt_bias.reshape(no, 2 * heads)[:, perm][:, :, None]
    ssd_alog = ssd_a_log.reshape(no, 2 * heads)[:, perm][:, :, None]
    ssd_dcol = jnp.repeat(ssd_d, hp, axis=1)[:, :, None]
    weights = (norm_g, fc_w_in.astype(BF16), fc_conv_w, fc_w_out.astype(BF16), ssd_w, ssd_wdt_t,
               ssd_dtb, ssd_alog, ssd_conv_wb, ssd_dcol, ssd_norm_g,
               ssd_w_out.astype(BF16), final_norm_g)

    y_prompt, states = _trunk(x_prompt, mod, lambda s: dec_batch, True, x_prompt.shape[1], None,
                              True, weights)
    y_sample, _ = _trunk(x_sample, mod, lambda s: s, False, GRID_W, state_ssm, False, weights)
    return y_prompt, y_sample, states
```

```python
import functools
import math

import jax
import jax.numpy as jnp
from jax import lax
from jax.experimental import pallas as pl
from jax.experimental.pallas import tpu as pltpu

F32 = jnp.float32
BF16 = jnp.bfloat16

EPS = 1e-6
GRID_W = 64
FOURIER_GROUPS = 4
DFT_INNER = 128
REVERSE_BLOCK = 128
F32_SUBLANES = 8
CHUNK = 128
HEADS_PER_GROUP = 4
LANES = 128
TOKEN_TILE = 512
FC_IN_TILE = 1024
SSD_OUT_TILE = 1024
SSD_IN_TOKENS = 2048
SSD_IN_ROWS = 1024
SSD_IN_SUB_ROWS = 256
SCAN_CELL_CHUNKS = 16
COND_ROWS = 16
VMEM_LIMIT = 56 * 1024 * 1024


def _params(n_grid):
    return pltpu.CompilerParams(
        dimension_semantics=("arbitrary",) * n_grid, vmem_limit_bytes=VMEM_LIMIT)


def _dot(a, b):
    return jnp.dot(a, b, preferred_element_type=F32)


def _dot_nt(a, b):
    return lax.dot_general(a, b, (((1,), (1,)), ((), ())), preferred_element_type=F32)


def _sigmoid(x):
    return 1.0 / (1.0 + jnp.exp(-x))


def _silu(x):
    return x * _sigmoid(x)


def _rms(x):
    return x * lax.rsqrt(jnp.mean(x * x, axis=-1, keepdims=True) + EPS)


def _modnorm(x, g_ref, mod_ref):
    d = x.shape[-1]
    shift = mod_ref[:, 0:d]
    scale = mod_ref[:, d:2 * d]
    return _rms(x) * (g_ref[...] * (1.0 + scale)) + shift


def _residual(x, out, mod_ref, fg_ref):
    d = x.shape[-1]
    xn = x + mod_ref[:, 2 * d:3 * d] * out
    if fg_ref is not None:
        xn = _rms(xn) * fg_ref[...]
    return xn


def _mod_kernel(c_ref, w_ref, b_ref, o_ref):
    act = _silu(c_ref[...]).astype(BF16)
    o_ref[...] = _dot(act, w_ref[...].astype(BF16)) + b_ref[...]


def _modulation(cond, w_mod, b_mod):
    depth, d, d3 = w_mod.shape
    nb = d3 // d
    return pl.pallas_call(
        _mod_kernel,
        grid=(depth, nb),
        in_specs=[
            pl.BlockSpec((COND_ROWS, d), lambda l, j: (0, 0)),
            pl.BlockSpec((None, d, d), lambda l, j: (l, 0, j)),
            pl.BlockSpec((None, 1, d), lambda l, j: (l, 0, j)),
        ],
        out_specs=pl.BlockSpec((None, COND_ROWS, d), lambda l, j: (l, 0, j)),
        out_shape=jax.ShapeDtypeStruct((depth, COND_ROWS, d3), F32),
        compiler_params=_params(2),
        name="modulation",
    )(cond, w_mod, b_mod.reshape(depth, 1, d3))


def _fc_in_kernel(x_ref, g_ref, mod_ref, w_ref, cw_ref, dft_ref,
                  xc_ref, xs_ref, ga_ref, yb_ref, *, row_len, col_block):
    tm, d = x_ref.shape
    gd = d // FOURIER_GROUPS
    hm = _modnorm(x_ref[...], g_ref, mod_ref).astype(BF16)
    ua = _dot(hm, w_ref[:, 0:d]).astype(BF16)
    for g in range(FOURIER_GROUPS):
        t = _dot(ua[:, g * gd:(g + 1) * gd], dft_ref[...])
        xc_ref[:, g * gd:(g + 1) * gd] = t[:, :gd].astype(BF16)
        xs_ref[:, g * gd:(g + 1) * gd] = t[:, gd:].astype(BF16)
    pos = lax.rem(lax.broadcasted_iota(jnp.int32, (tm, 1), 0), row_len)
    first = pos == 0
    last = pos == row_len - 1
    for j in range(d // col_block):
        lo = j * col_block
        za = _dot(hm, w_ref[:, d + lo:d + lo + col_block])
        ga_ref[:, lo:lo + col_block] = _silu(za).astype(BF16)
        bb = _dot(hm, w_ref[:, 2 * d + lo:2 * d + lo + col_block])
        cc = _dot(hm, w_ref[:, 3 * d + lo:3 * d + lo + col_block])
        vv = _dot(hm, w_ref[:, 4 * d + lo:4 * d + lo + col_block])
        zb = _dot(hm, w_ref[:, 5 * d + lo:5 * d + lo + col_block])
        u = cc * vv
        up = jnp.where(first, 0.0, pltpu.roll(u, 1, 0))
        un = jnp.where(last, 0.0, pltpu.roll(u, tm - 1, 0))
        cw = cw_ref[:, lo:lo + col_block]
        y = up * cw[0:1] + u * cw[1:2] + un * cw[2:3]
        yb_ref[:, lo:lo + col_block] = (bb * y * _silu(zb)).astype(BF16)


def _fc_out_kernel(dc_ref, ds_ref, dch_ref, rev_ref, xc_ref, xs_ref, ga_ref, yb_ref, w_ref, x_ref,
                   mod_ref, *rest, final, direct_tiles, mirror):
    fg_ref = rest[0] if final else None
    o_ref, xcf_s, xsf_s, d_s = rest[-4:]
    r = pl.program_id(1)
    tr, d = x_ref.shape
    half = xcf_s.shape[0]
    ln = 2 * half
    rb = rev_ref.shape[0]

    @pl.when(pl.program_id(1) == 0)
    def _():
        for b in range(half // rb):
            lo = ln - rb * b - rb

            def reversed_rows(ref):
                near = ref[lo:lo + rb, :]
                far = ref[lo + rb:lo + 2 * rb, :] if b > 0 else jnp.zeros_like(near)
                return _dot(rev_ref[...], jnp.concatenate([near, far], axis=0))

            rows = slice(rb * b, rb * (b + 1))
            xcf_s[rows, :] = (xc_ref[rows, :].astype(F32) + reversed_rows(xc_ref)).astype(BF16)
            xsf_s[rows, :] = (xs_ref[rows, :].astype(F32) - reversed_rows(xs_ref)).astype(BF16)
        if mirror:
            y_half = (_dot(dch_ref[...], xcf_s[...])[0:1]
                      + xc_ref[half:half + 1, :].astype(F32) * (1.0 / math.sqrt(ln)))
            first = lax.broadcasted_iota(jnp.int32, (rb, 1), 0) == 0
            d_s[half:half + rb, :] = jnp.where(first, y_half, 0.0).astype(BF16)

    def finish(ya):
        h1 = (ya * ga_ref[...].astype(F32)).astype(BF16)
        out = _dot(h1, w_ref[0:d, :]) + _dot(yb_ref[...], w_ref[d:2 * d, :])
        o_ref[...] = _residual(x_ref[...], out, mod_ref, fg_ref)

    def direct():
        odd = lax.broadcasted_iota(jnp.int32, (tr, 1), 0) & 1
        alt = jnp.where(odd == 1, -1.0, 1.0) * (1.0 / math.sqrt(ln))
        a = _dot(dc_ref[...], xcf_s[...]) + alt * xc_ref[half:half + 1, :].astype(F32)
        b = _dot(ds_ref[...], xsf_s[...])
        if mirror:
            d_s[pl.ds(pl.multiple_of(r * tr, tr), tr), :] = (a - b).astype(BF16)
        finish(a + b)

    if not mirror:
        direct()
    else:
        pl.when(r < direct_tiles)(direct)

        @pl.when(r >= direct_tiles)
        def _():
            m0 = (r - direct_tiles) * tr
            parts = []
            for q in range(tr // rb):
                lo = pl.multiple_of(half - m0 - (q + 1) * rb, rb)
                parts.append(_dot(rev_ref[...], d_s[pl.ds(lo, 2 * rb), :]))
            finish(jnp.concatenate(parts, axis=0))


def _dft_tables(n, dtype=BF16):
    nb = DFT_INNER if n % DFT_INNER == 0 else 1
    k = jnp.arange(n, dtype=jnp.int32)[:, None]

    def base(cols):
        ang = ((k * cols[None, :]) % n).astype(F32) * (2.0 * math.pi / n)
        return jnp.cos(ang), jnp.sin(ang)

    ca, sa = base(jnp.arange(n // nb, dtype=jnp.int32) * nb)
    cb, sb = base(jnp.arange(nb, dtype=jnp.int32))
    ca, sa, cb, sb = ca[:, :, None], sa[:, :, None], cb[:, None, :], sb[:, None, :]
    s = 1.0 / math.sqrt(n)
    cos = ((ca * cb - sa * sb) * s).reshape(n, n)
    sin = ((sa * cb + ca * sb) * s).reshape(n, n)
    return cos.astype(dtype), sin.astype(dtype)


def _token_tile(seq, t, cap):
    tile = min(cap, t if seq.shared_cond else seq.length)
    assert t % tile == 0 and (seq.length % tile == 0 or tile % seq.length == 0)
    return tile


def _fc_layer(x2d, seq, mod3, mod_row, norm_g, w_in, conv_w, w_out, li, row_len, final_g):
    t, d = x2d.shape
    tm = _token_tile(seq, t, FC_IN_TILE)
    assert tm % row_len == 0
    tiles_per_seq = max(seq.length // tm, 1)
    gd = d // FOURIER_GROUPS
    cc, sc = _dft_tables(gd)
    dft_ch = jnp.concatenate([cc, sc], axis=1)
    row_of_tile = lambda i: mod_row(i // tiles_per_seq)
    const = lambda *_: (0, 0)
    tok = pl.BlockSpec((tm, d), lambda i: (i, 0))
    xc, xs, ga, yb = pl.pallas_call(
        functools.partial(_fc_in_kernel, row_len=row_len, col_block=min(256, d)),
        grid=(t // tm,),
        in_specs=[
            tok,
            pl.BlockSpec((1, d), const),
            pl.BlockSpec((None, 1, 3 * d), lambda i: (row_of_tile(i), 0, 0)),
            pl.BlockSpec((None, d, 6 * d), lambda i: (li, 0, 0), pipeline_mode=pl.Buffered(1)),
            pl.BlockSpec((3, d), const),
            pl.BlockSpec((gd, 2 * gd), const),
        ],
        out_specs=[tok, tok, tok, tok],
        out_shape=[jax.ShapeDtypeStruct((t, d), BF16)] * 4,
        compiler_params=_params(1),
        name="fc_in",
    )(x2d, norm_g[None], mod3, w_in, conv_w, dft_ch)

    ln = seq.length
    tr = min(TOKEN_TILE, ln)
    rt = ln // tr
    cl, sl = _dft_tables(ln)
    final = final_g is not None
    row = pl.BlockSpec((tr, d), lambda s, r: (s * rt + r, 0))
    whole = pl.BlockSpec((ln, d), lambda s, r: (s, 0))
    half = ln // 2
    rb = min(REVERSE_BLOCK, half)
    assert ln % 2 == 0 and tr % 2 == 0 and half % rb == 0
    ri = jnp.arange(rb, dtype=jnp.int32)[:, None]
    rev = jnp.where(jnp.arange(2 * rb, dtype=jnp.int32)[None, :] == rb - ri, 1.0, 0.0).astype(BF16)
    mirror = rt % 2 == 0
    direct_tiles = rt // 2 if mirror else rt
    table = pl.BlockSpec((tr, half), lambda s, r: (jnp.minimum(r, direct_tiles - 1), 0))
    in_specs = [
        table, table,
        pl.BlockSpec((F32_SUBLANES, half), lambda s, r: (half // F32_SUBLANES, 0)),
        pl.BlockSpec((rb, 2 * rb), lambda s, r: (0, 0)),
        whole, whole, row, row,
        pl.BlockSpec((None, 2 * d, d), lambda s, r: (li, 0, 0), pipeline_mode=pl.Buffered(1)),
        row,
        pl.BlockSpec((None, 1, 3 * d), lambda s, r: (mod_row(s), 0, 0)),
    ]
    args = [cl, -sl, cl, rev, xc, xs, ga, yb, w_out, x2d, mod3]
    if final:
        in_specs.append(pl.BlockSpec((1, d), lambda s, r: (0, 0)))
        args.append(final_g[None])
    return pl.pallas_call(
        functools.partial(_fc_out_kernel, final=final, direct_tiles=direct_tiles, mirror=mirror),
        grid=(seq.count, rt),
        in_specs=in_specs,
        out_specs=row,
        out_shape=jax.ShapeDtypeStruct((t, d), F32),
        scratch_shapes=[pltpu.VMEM((half, d), BF16), pltpu.VMEM((half, d), BF16),
                        pltpu.VMEM((half + rb, d), BF16)],
        compiler_params=_params(2),
        name="fc_out",
    )(*args)


TAB_DT, TAB_CUM2, TAB_ROW2, TAB_COEF, TAB_E, TAB_CD = range(6)
LOG2_E = math.log2(math.e)
LOG2_FLOOR = -1e30


def _ssd_in_kernel(x_ref, g_ref, mod_ref, wt_ref, wdt_ref, dtb_ref, alog_ref, cw_ref,
                   proj_ref, tab_ref, hm_s, *, seq_len, gate_blocks, sub_rows):
    j = pl.program_id(1)
    nt = x_ref.shape[0]
    rows = wt_ref.shape[0]
    nc = nt // CHUNK

    def emit(r0, y):
        yb = y.astype(BF16)
        for k in range(nc):
            proj_ref[k, pl.ds(r0, sub_rows), :] = yb[:, k * CHUNK:(k + 1) * CHUNK]

    @pl.when(j == 0)
    def _():
        hm = _modnorm(x_ref[...], g_ref, mod_ref).astype(BF16)
        hm_s[...] = hm
        _scan_tables(hm, wdt_ref, dtb_ref, alog_ref, tab_ref)

    @pl.when(j < gate_blocks)
    def _():
        for r0 in range(0, rows, sub_rows):
            emit(r0, _silu(_dot_nt(wt_ref[r0:r0 + sub_rows, :], hm_s[...])))

    @pl.when(j >= gate_blocks)
    def _():
        lane0 = lax.broadcasted_iota(jnp.int32, (sub_rows, LANES), 1)

        @pl.loop(0, rows // sub_rows)
        def _(i):
            r0 = pl.multiple_of(i * sub_rows, sub_rows)
            p = _dot_nt(wt_ref[pl.ds(r0, sub_rows), :], hm_s[...])
            left = pltpu.roll(p, 1, 1)
            right = pltpu.roll(p, nt - 1, 1)
            lcols = [left[:, q:q + LANES] for q in range(0, nt, LANES)]
            rcols = [right[:, q:q + LANES] for q in range(0, nt, LANES)]
            for q in range(0, nt, seq_len):
                lcols[q // LANES] = jnp.where(lane0 == 0, 0.0, lcols[q // LANES])
                e = (q + seq_len) // LANES - 1
                rcols[e] = jnp.where(lane0 == LANES - 1, 0.0, rcols[e])
            left = jnp.concatenate(lcols, axis=1)
            right = jnp.concatenate(rcols, axis=1)
            cw = cw_ref[pl.ds(r0, sub_rows), :]
            y = left * cw[:, 0:1] + p * cw[:, 1:2] + right * cw[:, 2:3] + cw[:, 3:4]
            emit(r0, _silu(y))


def _scan_tables(hm, wdt_ref, dtb_ref, alog_ref, tab_ref):
    v = _dot_nt(wdt_ref[...], hm) + dtb_ref[...]
    dt_all = jnp.maximum(v, 0.0) + jnp.log(1.0 + jnp.exp(-jnp.abs(v)))
    heads2, nt = dt_all.shape
    nc = nt // CHUNK
    la_all = dt_all * -jnp.exp(alog_ref[...])
    dt = jnp.concatenate([dt_all[:, k * CHUNK:(k + 1) * CHUNK] for k in range(nc)], axis=0)
    la = jnp.concatenate([la_all[:, k * CHUNK:(k + 1) * CHUNK] for k in range(nc)], axis=0)
    p0 = la.astype(BF16)
    r1 = la - p0.astype(F32)
    p1 = r1.astype(BF16)
    p2 = (r1 - p1.astype(F32)).astype(BF16)
    ji = lax.broadcasted_iota(jnp.int32, (CHUNK, CHUNK), 0)
    li = lax.broadcasted_iota(jnp.int32, (CHUNK, CHUNK), 1)
    upto = jnp.where(ji <= li, 1.0, 0.0).astype(BF16)
    from_ = jnp.where(ji >= li, 1.0, 0.0).astype(BF16)
    pre = _dot(p0, upto) + _dot(p1, upto) + _dot(p2, upto)
    suf = _dot(p0, from_) + _dot(p1, from_) + _dot(p2, from_)
    row = lax.broadcasted_iota(jnp.int32, la.shape, 0)
    is_fwd = (row & (2 * HEADS_PER_GROUP - 1)) < HEADS_PER_GROUP
    cum = jnp.where(is_fwd, pre, suf)
    tot = jnp.where(is_fwd, jnp.broadcast_to(cum[:, CHUNK - 1:CHUNK], cum.shape),
                    jnp.broadcast_to(cum[:, 0:1], cum.shape))
    coef = dt * jnp.exp(tot - cum)
    e = jnp.exp(cum)
    cd = jnp.exp(tot)
    cum2 = cum * LOG2_E
    row2 = cum2 - jnp.maximum(jnp.log2(dt), LOG2_FLOOR)
    for k in range(nc):
        rows = slice(k * heads2, (k + 1) * heads2)
        tab_ref[k, TAB_DT] = dt[rows]
        tab_ref[k, TAB_CUM2] = cum2[rows]
        tab_ref[k, TAB_ROW2] = row2[rows]
        tab_ref[k, TAB_COEF] = coef[rows]
        tab_ref[k, TAB_E] = e[rows]
        tab_ref[k, TAB_CD] = cd[rows]


def _ssd_scan_kernel(sz_ref, x_ref, b_ref, c_ref, tab_ref, dcol_ref, gcol_ref,
                     *rest, seq_chunks, has_h0, has_prev, emit_state, unroll):
    rest = list(rest)
    h0_ref = rest.pop(0) if has_h0 else None
    prev_ref = rest.pop(0) if has_prev else None
    y_ref = rest.pop(0)
    st_ref = rest.pop(0) if emit_state else None
    yp_s, stf_s, stb_s, hf_s, hb_s, seg_s, cb_s, bm_s = rest

    nc, gw, _ = x_ref.shape
    n = b_ref.shape[1]
    hp = gw // HEADS_PER_GROUP
    nh = HEADS_PER_GROUP
    r2 = 2 * nh
    assert not has_h0 or nc == seq_chunks

    li = lax.broadcasted_iota(jnp.int32, (CHUNK, CHUNK), 0)
    si = lax.broadcasted_iota(jnp.int32, (CHUNK, CHUNK), 1)

    def expand(rows4):
        return jnp.concatenate(
            [jnp.broadcast_to(rows4[h:h + 1, :], (hp, LANES)) for h in range(nh)], axis=0)

    blk = (lax.broadcasted_iota(jnp.int32, (r2, r2 * CHUNK), 1) // CHUNK
           == lax.broadcasted_iota(jnp.int32, (r2, r2 * CHUNK), 0))
    sel_k = jnp.concatenate([jnp.where(blk, 1.0, 0.0)] * 3, axis=0)
    ones_k = jnp.ones((3 * r2, CHUNK), F32)

    def split3(v):
        p0 = v.astype(BF16).astype(F32)
        p1 = (v - p0).astype(BF16).astype(F32)
        return p0, p1, v - p0 - p1

    def seg_chunk(c, carry):
        lhs = jnp.concatenate([-p for p in split3(tab_ref[c, TAB_ROW2])] + [ones_k], axis=0)
        rhs = jnp.concatenate(
            [sel_k] + [jnp.where(blk, jnp.concatenate([p] * r2, axis=1), 0.0)
                       for p in split3(tab_ref[c, TAB_CUM2])], axis=0)
        seg_s[c] = lax.dot_general(lhs.astype(BF16), rhs.astype(BF16), (((0,), (0,)), ((), ())),
                                   preferred_element_type=F32)
        bm = b_ref[c].T
        bm_s[c] = bm
        cb_s[c] = _dot(bm, c_ref[c])
        return carry

    lax.fori_loop(0, nc, seg_chunk, 0, unroll=unroll)

    def local_chunk(c, carry):
        xst_b = x_ref[c]
        xst = xst_b.astype(F32)
        cb = cb_s[c]
        parts = []
        for h in range(nh):
            hb = nh + h
            arg = jnp.where(li <= si, seg_s[c, :, h * CHUNK:(h + 1) * CHUNK],
                            seg_s[c, :, hb * CHUNK:(hb + 1) * CHUNK])
            w = (cb * jnp.exp2(arg)).astype(BF16)
            parts.append(_dot(xst_b[h * hp:(h + 1) * hp, :], w))
        cb_diag = jnp.sum(jnp.where(si == li, cb, 0.0), axis=0, keepdims=True)
        skip = dcol_ref[...] + expand(tab_ref[c, TAB_DT][nh:r2] * cb_diag)
        yp_s[c] = jnp.concatenate(parts, axis=0) + skip * xst

        coef8 = tab_ref[c, TAB_COEF]
        xdw = jnp.concatenate([xst * expand(coef8[0:nh]), xst * expand(coef8[nh:r2])], axis=0)
        st = _dot(xdw.astype(BF16), bm_s[c])
        stf_s[c] = st[0:gw]
        stb_s[c] = st[gw:2 * gw]
        return carry

    lax.fori_loop(0, nc, local_chunk, 0, unroll=unroll)

    if has_h0:
        h0f = h0_ref[0].reshape(gw, n)
        h0b = h0_ref[1].reshape(gw, n)
    else:
        h0f = jnp.zeros((gw, n), F32)
        h0b = h0f
    if has_prev:
        st_ref[:, 0:prev_ref.shape[1]] = prev_ref[...]
    slot = st_ref.shape[1] - 1 if emit_state else None

    def fwd_state(c, h):
        hf_s[c] = h.astype(BF16)
        return h * expand(tab_ref[c, TAB_CD][0:nh]) + stf_s[c]

    def bwd_state(c, h):
        hb_s[c] = h.astype(BF16)
        return h * expand(tab_ref[c, TAB_CD][nh:r2]) + stb_s[c]

    for q in range(nc // seq_chunks):
        c0, c1 = q * seq_chunks, (q + 1) * seq_chunks
        hf = lax.fori_loop(c0, c1, fwd_state, h0f)
        hb = lax.fori_loop(0, seq_chunks, lambda i, h: bwd_state(c1 - 1 - i, h), h0b)
        if emit_state:
            st_ref[q, slot, 0] = hf.reshape(nh, hp, n)
            st_ref[q, slot, 1] = hb.reshape(nh, hp, n)

    def output_chunk(c, carry):
        hin = jnp.concatenate([hf_s[c], hb_s[c]], axis=0)
        yo = _dot(hin, c_ref[c])
        e8 = tab_ref[c, TAB_E]
        y_t = yp_s[c] + yo[0:gw] * expand(e8[0:nh]) + yo[gw:2 * gw] * expand(e8[nh:r2])
        y = y_t * sz_ref[c].astype(F32)
        inv = lax.rsqrt(jnp.mean(y * y, axis=0, keepdims=True) + EPS)
        y_ref[c] = (y * inv * gcol_ref[...]).astype(BF16)
        return carry

    lax.fori_loop(0, nc, output_chunk, 0, unroll=unroll)


def _ssd_out_kernel(y_ref, w_ref, x_ref, mod_ref, *rest, final):
    fg_ref = rest[0] if final else None
    o_ref = rest[-1]
    y = jnp.concatenate([y_ref[k].T for k in range(y_ref.shape[0])], axis=0)
    out = _dot(y, w_ref[...])
    o_ref[...] = _residual(x_ref[...], out, mod_ref, fg_ref)


def _ssd_layer(x2d, seq, mod3, mod_row, norm_g, w_zxbc, wdt_t, dt_bias_col, alog_col,
               conv_wb, d_col, gn_g, w_out, h0, prev_state, layer_idx, emit_state, final_g):
    t, d = x2d.shape
    d_inner = w_out.shape[1]
    heads2 = wdt_t.shape[1]
    groups = heads2 // (2 * HEADS_PER_GROUP)
    gw = d_inner // groups
    n = (w_zxbc.shape[1] - 2 * d_inner) // (2 * groups)
    hp = gw // HEADS_PER_GROUP
    ln = seq.length
    nc = ln // CHUNK
    tm = _token_tile(seq, t, SSD_OUT_TILE)
    tiles_per_seq = max(ln // tm, 1)
    row_of_tile = lambda i: mod_row(i // tiles_per_seq)
    const = lambda *_: (0, 0)
    tok = pl.BlockSpec((tm, d), lambda i: (i, 0))
    wrows = w_zxbc.shape[1]

    nt = ln * max(1, min(SSD_IN_TOKENS, t) // ln) if seq.shared_cond else ln
    seqs_per_block = nt // ln
    rows_blk = min(SSD_IN_ROWS, d_inner)
    assert t % nt == 0 and d_inner % rows_blk == 0 and wrows % rows_blk == 0
    n_tab = TAB_CD + 1
    gate_blocks = d_inner // rows_blk
    proj, tab = pl.pallas_call(
        functools.partial(_ssd_in_kernel, seq_len=ln, gate_blocks=gate_blocks,
                          sub_rows=min(SSD_IN_SUB_ROWS, rows_blk)),
        grid=(t // nt, wrows // rows_blk),
        in_specs=[
            pl.BlockSpec((nt, d), lambda i, j: (i, 0)),
            pl.BlockSpec((1, d), const),
            pl.BlockSpec((None, 1, 3 * d), lambda i, j: (mod_row(i * seqs_per_block), 0, 0)),
            pl.BlockSpec((None, rows_blk, d), lambda i, j: (layer_idx, j, 0)),
            pl.BlockSpec((None, heads2, d), lambda i, j: (layer_idx, 0, 0)),
            pl.BlockSpec((heads2, 1), const),
            pl.BlockSpec((heads2, 1), const),
            pl.BlockSpec((rows_blk, 4), lambda i, j: (jnp.maximum(j - gate_blocks, 0), 0)),
        ],
        out_specs=[
            pl.BlockSpec((nt // CHUNK, rows_blk, CHUNK), lambda i, j: (i, j, 0)),
            pl.BlockSpec((nt // CHUNK, n_tab, heads2, CHUNK), lambda i, j: (i, 0, 0, 0)),
        ],
        out_shape=[
            jax.ShapeDtypeStruct((t // CHUNK, wrows, CHUNK), BF16),
            jax.ShapeDtypeStruct((t // CHUNK, n_tab, heads2, CHUNK), F32),
        ],
        scratch_shapes=[pltpu.VMEM((nt, d), BF16)],
        compiler_params=_params(2),
        name="ssd_in",
    )(x2d, norm_g[None], mod3, w_zxbc, wdt_t, dt_bias_col, alog_col, conv_wb)

    r2 = 2 * HEADS_PER_GROUP
    xg0 = d_inner // gw
    bg0 = 2 * d_inner // n
    cg0 = bg0 + groups
    has_h0 = h0 is not None
    has_prev = emit_state and prev_state is not None
    spc = 1 if has_h0 else max(1, min(seq.count, SCAN_CELL_CHUNKS // nc))
    assert seq.count % spc == 0
    cc = spc * nc
    nh = HEADS_PER_GROUP
    in_specs = [
        pl.BlockSpec((cc, gw, CHUNK), lambda s, g: (s, g, 0)),
        pl.BlockSpec((cc, gw, CHUNK), lambda s, g: (s, xg0 + g, 0)),
        pl.BlockSpec((cc, n, CHUNK), lambda s, g: (s, bg0 + g, 0)),
        pl.BlockSpec((cc, n, CHUNK), lambda s, g: (s, cg0 + g, 0)),
        pl.BlockSpec((cc, n_tab, r2, CHUNK), lambda s, g: (s, 0, g, 0)),
        pl.BlockSpec((gw, LANES), lambda s, g: (g, 0)),
        pl.BlockSpec((gw, LANES), lambda s, g: (g, 0)),
    ]
    args = [proj, proj, proj, proj, tab, jnp.broadcast_to(d_col, (d_inner, LANES)),
            jnp.broadcast_to(gn_g[:, None], (d_inner, LANES))]
    if has_h0:
        in_specs.append(pl.BlockSpec((None, None, 2, nh, hp, n),
                                     lambda s, g: (s, layer_idx, 0, g, 0, 0)))
        args.append(h0)
    if has_prev:
        in_specs.append(pl.BlockSpec((spc, prev_state.shape[1], 2, nh, hp, n),
                                     lambda s, g: (s, 0, 0, g, 0, 0)))
        args.append(prev_state)
    out_specs = [pl.BlockSpec((cc, gw, CHUNK), lambda s, g: (s, g, 0))]
    out_shape = [jax.ShapeDtypeStruct((t // CHUNK, d_inner, CHUNK), BF16)]
    if emit_state:
        slots = (prev_state.shape[1] if has_prev else 0) + 1
        out_specs.append(pl.BlockSpec((spc, slots, 2, nh, hp, n), lambda s, g: (s, 0, 0, g, 0, 0)))
        out_shape.append(jax.ShapeDtypeStruct((seq.count, slots, 2, groups * nh, hp, n), F32))
    res = pl.pallas_call(
        functools.partial(_ssd_scan_kernel, seq_chunks=nc, has_h0=has_h0, has_prev=has_prev,
                          emit_state=emit_state, unroll=min(16, cc)),
        grid=(seq.count // spc, groups),
        in_specs=in_specs,
        out_specs=out_specs,
        out_shape=out_shape,
        scratch_shapes=[
            pltpu.VMEM((cc, gw, CHUNK), F32),
            pltpu.VMEM((cc, gw, n), F32),
            pltpu.VMEM((cc, gw, n), F32),
            pltpu.VMEM((cc, gw, n), BF16),
            pltpu.VMEM((cc, gw, n), BF16),
            pltpu.VMEM((cc, CHUNK, r2 * CHUNK), F32),
            pltpu.VMEM((cc, CHUNK, CHUNK), F32),
            pltpu.VMEM((cc, CHUNK, n), BF16),
        ],
        compiler_params=_params(2),
        name="ssd_scan",
    )(*args)
    y, state = (res[0], res[1]) if emit_state else (res[0], None)

    final = final_g is not None
    in_specs = [
        pl.BlockSpec((tm // CHUNK, d_inner, CHUNK), lambda i: (i, 0, 0)),
        pl.BlockSpec((None, d_inner, d), lambda i: (layer_idx, 0, 0), pipeline_mode=pl.Buffered(1)),
        tok,
        pl.BlockSpec((None, 1, 3 * d), lambda i: (row_of_tile(i), 0, 0)),
    ]
    args = [y, w_out, x2d, mod3]
    if final:
        in_specs.append(pl.BlockSpec((1, d), const))
        args.append(final_g[None])
    x_new = pl.pallas_call(
        functools.partial(_ssd_out_kernel, final=final),
        grid=(t // tm,),
        in_specs=in_specs,
        out_specs=tok,
        out_shape=jax.ShapeDtypeStruct((t, d), F32),
        compiler_params=_params(1),
        name="ssd_out",
    )(*args)
    return x_new, state


class _Seqs:
    def __init__(self, count, length, shared_cond):
        self.count = count
        self.length = length
        self.shared_cond = shared_cond


def _trunk(x, mod, mod_row, shared_cond, row_len, h0, emit_state, weights):
    (norm_g, fc_w_in, fc_conv_w, fc_w_out, ssd_w, ssd_wdt_t, ssd_dtb, ssd_alog, ssd_conv_wb,
     ssd_dcol, ssd_norm_g, ssd_w_out, final_norm_g) = weights
    bsz, ln, d = x.shape
    seq = _Seqs(bsz, ln, shared_cond)
    x2d = x.reshape(bsz * ln, d)
    depth = norm_g.shape[0]
    states = None
    for layer in range(depth):
        i = layer // 2
        mod3 = mod[layer].reshape(COND_ROWS, 1, 3 * d)
        final_g = final_norm_g if layer == depth - 1 else None
        if layer % 2 == 0:
            x2d = _fc_layer(x2d, seq, mod3, mod_row, norm_g[layer], fc_w_in, fc_conv_w[i],
                            fc_w_out, i, row_len, final_g)
        else:
            x2d, st = _ssd_layer(x2d, seq, mod3, mod_row, norm_g[layer], ssd_w, ssd_wdt_t,
                                 ssd_dtb[i], ssd_alog[i], ssd_conv_wb[i], ssd_dcol[i],
                                 ssd_norm_g[i], ssd_w_out, h0, states, i, emit_state, final_g)
            states = st
    return x2d.reshape(bsz, ln, d), states


def kernel(x_prompt, x_sample, state_ssm, c, c_ctx, w_mod, b_mod, norm_g, fc_w_in, fc_conv_w,
           fc_w_out, ssd_w_in, ssd_conv_w, ssd_conv_b, ssd_dt_bias, ssd_a_log, ssd_d,
           ssd_norm_g, ssd_w_out, final_norm_g):
    d = x_prompt.shape[-1]
    dec_batch = x_sample.shape[0]
    no, _, heads = ssd_a_log.shape
    d_inner = ssd_w_out.shape[1]
    conv_dim = ssd_conv_w.shape[-1]
    hp = d_inner // heads
    groups = heads // HEADS_PER_GROUP
    assert dec_batch + 1 <= COND_ROWS and heads % HEADS_PER_GROUP == 0

    cond = jnp.zeros((COND_ROWS, d), F32).at[:dec_batch].set(c).at[dec_batch].set(c_ctx)
    mod = _modulation(cond, w_mod, b_mod)

    zx = d_inner + conv_dim
    ssd_w = jnp.swapaxes(ssd_w_in[:, :, :zx], 1, 2).astype(BF16)
    ssd_conv_wb = jnp.swapaxes(jnp.concatenate([ssd_conv_w, ssd_conv_b[:, None]], axis=1), 1, 2)
    perm = jnp.arange(2 * heads).reshape(2, groups, HEADS_PER_GROUP).transpose(1, 0, 2).reshape(-1)
    ssd_wdt_t = jnp.swapaxes(ssd_w_in[:, :, zx:][:, :, perm], 1, 2).astype(BF16)
    ssd_dtb = ssd_dt_bias.reshape(no, 2 * heads)[:, perm][:, :, None]
    ssd_alog = ssd_a_log.reshape(no, 2 * heads)[:, perm][:, :, None]
    ssd_dcol = jnp.repeat(ssd_d, hp, axis=1)[:, :, None]
    weights = (norm_g, fc_w_in.astype(BF16), fc_conv_w, fc_w_out.astype(BF16), ssd_w, ssd_wdt_t,
               ssd_dtb, ssd_alog, ssd_conv_wb, ssd_dcol, ssd_norm_g,
               ssd_w_out.astype(BF16), final_norm_g)

    y_prompt, states = _trunk(x_prompt, mod, lambda s: dec_batch, True, x_prompt.shape[1], None,
                              True, weights)
    y_sample, _ = _trunk(x_sample, mod, lambda s: s, False, GRID_W, state_ssm, False, weights)
    return y_prompt, y_sample, states
```

```python
import functools
import math

import jax
import jax.numpy as jnp
from jax import lax
from jax.experimental import pallas as pl
from jax.experimental.pallas import tpu as pltpu

F32 = jnp.float32
BF16 = jnp.bfloat16

EPS = 1e-6
GRID_W = 64
FOURIER_GROUPS = 4
DFT_INNER = 128
REVERSE_BLOCK = 128
F32_SUBLANES = 8
CHUNK = 128
HEADS_PER_GROUP = 4
LANES = 128
TOKEN_TILE = 512
FC_IN_TILE = 1024
SSD_OUT_TILE = 1024
SSD_IN_TOKENS = 2048
SSD_IN_ROWS = 1024
SSD_IN_SUB_ROWS = 256
SCAN_CELL_CHUNKS = 16
COND_ROWS = 16
VMEM_LIMIT = 56 * 1024 * 1024


def _params(n_grid):
    return pltpu.CompilerParams(
        dimension_semantics=("arbitrary",) * n_grid, vmem_limit_bytes=VMEM_LIMIT)


def _dot(a, b):
    return jnp.dot(a, b, preferred_element_type=F32)


def _dot_nt(a, b):
    return lax.dot_general(a, b, (((1,), (1,)), ((), ())), preferred_element_type=F32)


def _sigmoid(x):
    return 1.0 / (1.0 + jnp.exp(-x))


def _silu(x):
    h = 0.5 * x
    return h + h * jnp.tanh(h)


def _rms(x):
    return x * lax.rsqrt(jnp.mean(x * x, axis=-1, keepdims=True) + EPS)


def _modnorm(x, g_ref, mod_ref):
    d = x.shape[-1]
    shift = mod_ref[:, 0:d]
    scale = mod_ref[:, d:2 * d]
    return _rms(x) * (g_ref[...] * (1.0 + scale)) + shift


def _residual(x, out, mod_ref, fg_ref):
    d = x.shape[-1]
    xn = x + mod_ref[:, 2 * d:3 * d] * out
    if fg_ref is not None:
        xn = _rms(xn) * fg_ref[...]
    return xn


def _mod_kernel(c_ref, w_ref, b_ref, o_ref):
    act = _silu(c_ref[...]).astype(BF16)
    o_ref[...] = _dot(act, w_ref[...].astype(BF16)) + b_ref[...]


def _modulation(cond, w_mod, b_mod):
    depth, d, d3 = w_mod.shape
    nb = d3 // d
    return pl.pallas_call(
        _mod_kernel,
        grid=(depth, nb),
        in_specs=[
            pl.BlockSpec((COND_ROWS, d), lambda l, j: (0, 0)),
            pl.BlockSpec((None, d, d), lambda l, j: (l, 0, j)),
            pl.BlockSpec((None, 1, d), lambda l, j: (l, 0, j)),
        ],
        out_specs=pl.BlockSpec((None, COND_ROWS, d), lambda l, j: (l, 0, j)),
        out_shape=jax.ShapeDtypeStruct((depth, COND_ROWS, d3), F32),
        compiler_params=_params(2),
        name="modulation",
    )(cond, w_mod, b_mod.reshape(depth, 1, d3))


def _fc_in_kernel(x_ref, g_ref, mod_ref, w_ref, cw_ref, dft_ref,
                  xc_ref, xs_ref, ga_ref, yb_ref, *, row_len, col_block):
    tm, d = x_ref.shape
    gd = d // FOURIER_GROUPS
    hm = _modnorm(x_ref[...], g_ref, mod_ref).astype(BF16)
    ua = _dot(hm, w_ref[:, 0:d]).astype(BF16)
    for g in range(FOURIER_GROUPS):
        t = _dot(ua[:, g * gd:(g + 1) * gd], dft_ref[...])
        xc_ref[:, g * gd:(g + 1) * gd] = t[:, :gd].astype(BF16)
        xs_ref[:, g * gd:(g + 1) * gd] = t[:, gd:].astype(BF16)
    pos = lax.rem(lax.broadcasted_iota(jnp.int32, (tm, 1), 0), row_len)
    first = pos == 0
    last = pos == row_len - 1
    for j in range(d // col_block):
        lo = j * col_block
        za = _dot(hm, w_ref[:, d + lo:d + lo + col_block])
        ga_ref[:, lo:lo + col_block] = _silu(za).astype(BF16)
        bb = _dot(hm, w_ref[:, 2 * d + lo:2 * d + lo + col_block])
        cc = _dot(hm, w_ref[:, 3 * d + lo:3 * d + lo + col_block])
        vv = _dot(hm, w_ref[:, 4 * d + lo:4 * d + lo + col_block])
        zb = _dot(hm, w_ref[:, 5 * d + lo:5 * d + lo + col_block])
        u = cc * vv
        up = jnp.where(first, 0.0, pltpu.roll(u, 1, 0))
        un = jnp.where(last, 0.0, pltpu.roll(u, tm - 1, 0))
        cw = cw_ref[:, lo:lo + col_block]
        y = up * cw[0:1] + u * cw[1:2] + un * cw[2:3]
        yb_ref[:, lo:lo + col_block] = (bb * y * _silu(zb)).astype(BF16)


def _fc_out_kernel(dc_ref, ds_ref, dch_ref, rev_ref, xc_ref, xs_ref, ga_ref, yb_ref, w_ref, x_ref,
                   mod_ref, *rest, final, direct_tiles, mirror):
    fg_ref = rest[0] if final else None
    o_ref, xcf_s, xsf_s, d_s = rest[-4:]
    r = pl.program_id(1)
    tr, d = x_ref.shape
    half = xcf_s.shape[0]
    ln = 2 * half
    rb = rev_ref.shape[0]

    @pl.when(pl.program_id(1) == 0)
    def _():
        for b in range(half // rb):
            lo = ln - rb * b - rb

            def reversed_rows(ref):
                near = ref[lo:lo + rb, :]
                far = ref[lo + rb:lo + 2 * rb, :] if b > 0 else jnp.zeros_like(near)
                return _dot(rev_ref[...], jnp.concatenate([near, far], axis=0))

            rows = slice(rb * b, rb * (b + 1))
            xcf_s[rows, :] = (xc_ref[rows, :].astype(F32) + reversed_rows(xc_ref)).astype(BF16)
            xsf_s[rows, :] = (xs_ref[rows, :].astype(F32) - reversed_rows(xs_ref)).astype(BF16)
        if mirror:
            y_half = (_dot(dch_ref[...], xcf_s[...])[0:1]
                      + xc_ref[half:half + 1, :].astype(F32) * (1.0 / math.sqrt(ln)))
            first = lax.broadcasted_iota(jnp.int32, (rb, 1), 0) == 0
            d_s[half:half + rb, :] = jnp.where(first, y_half, 0.0).astype(BF16)

    def finish(ya):
        h1 = (ya * ga_ref[...].astype(F32)).astype(BF16)
        out = _dot(h1, w_ref[0:d, :]) + _dot(yb_ref[...], w_ref[d:2 * d, :])
        o_ref[...] = _residual(x_ref[...], out, mod_ref, fg_ref)

    def direct():
        odd = lax.broadcasted_iota(jnp.int32, (tr, 1), 0) & 1
        alt = jnp.where(odd == 1, -1.0, 1.0) * (1.0 / math.sqrt(ln))
        a = _dot(dc_ref[...], xcf_s[...]) + alt * xc_ref[half:half + 1, :].astype(F32)
        b = _dot(ds_ref[...], xsf_s[...])
        if mirror:
            d_s[pl.ds(pl.multiple_of(r * tr, tr), tr), :] = (a - b).astype(BF16)
        finish(a + b)

    if not mirror:
        direct()
    else:
        pl.when(r < direct_tiles)(direct)

        @pl.when(r >= direct_tiles)
        def _():
            m0 = (r - direct_tiles) * tr
            parts = []
            for q in range(tr // rb):
                lo = pl.multiple_of(half - m0 - (q + 1) * rb, rb)
                parts.append(_dot(rev_ref[...], d_s[pl.ds(lo, 2 * rb), :]))
            finish(jnp.concatenate(parts, axis=0))


def _dft_tables(n, dtype=BF16):
    nb = DFT_INNER if n % DFT_INNER == 0 else 1
    k = jnp.arange(n, dtype=jnp.int32)[:, None]

    def base(cols):
        ang = ((k * cols[None, :]) % n).astype(F32) * (2.0 * math.pi / n)
        return jnp.cos(ang), jnp.sin(ang)

    ca, sa = base(jnp.arange(n // nb, dtype=jnp.int32) * nb)
    cb, sb = base(jnp.arange(nb, dtype=jnp.int32))
    ca, sa, cb, sb = ca[:, :, None], sa[:, :, None], cb[:, None, :], sb[:, None, :]
    s = 1.0 / math.sqrt(n)
    cos = ((ca * cb - sa * sb) * s).reshape(n, n)
    sin = ((sa * cb + ca * sb) * s).reshape(n, n)
    return cos.astype(dtype), sin.astype(dtype)


def _token_tile(seq, t, cap):
    tile = min(cap, t if seq.shared_cond else seq.length)
    assert t % tile == 0 and (seq.length % tile == 0 or tile % seq.length == 0)
    return tile


def _fc_layer(x2d, seq, mod3, mod_row, norm_g, w_in, conv_w, w_out, li, row_len, final_g):
    t, d = x2d.shape
    tm = _token_tile(seq, t, FC_IN_TILE)
    assert tm % row_len == 0
    tiles_per_seq = max(seq.length // tm, 1)
    gd = d // FOURIER_GROUPS
    cc, sc = _dft_tables(gd)
    dft_ch = jnp.concatenate([cc, sc], axis=1)
    row_of_tile = lambda i: mod_row(i // tiles_per_seq)
    const = lambda *_: (0, 0)
    tok = pl.BlockSpec((tm, d), lambda i: (i, 0))
    xc, xs, ga, yb = pl.pallas_call(
        functools.partial(_fc_in_kernel, row_len=row_len, col_block=min(256, d)),
        grid=(t // tm,),
        in_specs=[
            tok,
            pl.BlockSpec((1, d), const),
            pl.BlockSpec((None, 1, 3 * d), lambda i: (row_of_tile(i), 0, 0)),
            pl.BlockSpec((None, d, 6 * d), lambda i: (li, 0, 0), pipeline_mode=pl.Buffered(1)),
            pl.BlockSpec((3, d), const),
            pl.BlockSpec((gd, 2 * gd), const),
        ],
        out_specs=[tok, tok, tok, tok],
        out_shape=[jax.ShapeDtypeStruct((t, d), BF16)] * 4,
        compiler_params=_params(1),
        name="fc_in",
    )(x2d, norm_g[None], mod3, w_in, conv_w, dft_ch)

    ln = seq.length
    tr = min(TOKEN_TILE, ln)
    rt = ln // tr
    cl, sl = _dft_tables(ln)
    final = final_g is not None
    row = pl.BlockSpec((tr, d), lambda s, r: (s * rt + r, 0))
    whole = pl.BlockSpec((ln, d), lambda s, r: (s, 0))
    half = ln // 2
    rb = min(REVERSE_BLOCK, half)
    assert ln % 2 == 0 and tr % 2 == 0 and half % rb == 0
    ri = jnp.arange(rb, dtype=jnp.int32)[:, None]
    rev = jnp.where(jnp.arange(2 * rb, dtype=jnp.int32)[None, :] == rb - ri, 1.0, 0.0).astype(BF16)
    mirror = rt % 2 == 0
    direct_tiles = rt // 2 if mirror else rt
    table = pl.BlockSpec((tr, half), lambda s, r: (jnp.minimum(r, direct_tiles - 1), 0))
    in_specs = [
        table, table,
        pl.BlockSpec((F32_SUBLANES, half), lambda s, r: (half // F32_SUBLANES, 0)),
        pl.BlockSpec((rb, 2 * rb), lambda s, r: (0, 0)),
        whole, whole, row, row,
        pl.BlockSpec((None, 2 * d, d), lambda s, r: (li, 0, 0), pipeline_mode=pl.Buffered(1)),
        row,
        pl.BlockSpec((None, 1, 3 * d), lambda s, r: (mod_row(s), 0, 0)),
    ]
    args = [cl, -sl, cl, rev, xc, xs, ga, yb, w_out, x2d, mod3]
    if final:
        in_specs.append(pl.BlockSpec((1, d), lambda s, r: (0, 0)))
        args.append(final_g[None])
    return pl.pallas_call(
        functools.partial(_fc_out_kernel, final=final, direct_tiles=direct_tiles, mirror=mirror),
        grid=(seq.count, rt),
        in_specs=in_specs,
        out_specs=row,
        out_shape=jax.ShapeDtypeStruct((t, d), F32),
        scratch_shapes=[pltpu.VMEM((half, d), BF16), pltpu.VMEM((half, d), BF16),
                        pltpu.VMEM((half + rb, d), BF16)],
        compiler_params=_params(2),
        name="fc_out",
    )(*args)


TAB_DT, TAB_CUM2, TAB_ROW2, TAB_COEF, TAB_E, TAB_CD = range(6)
LOG2_E = math.log2(math.e)
LOG2_FLOOR = -1e30


def _ssd_in_kernel(x_ref, g_ref, mod_ref, wt_ref, wdt_ref, dtb_ref, alog_ref, cw_ref,
                   proj_ref, tab_ref, hm_s, *, seq_len, gate_blocks, sub_rows):
    j = pl.program_id(1)
    nt = x_ref.shape[0]
    rows = wt_ref.shape[0]
    nc = nt // CHUNK

    def emit(r0, y):
        yb = y.astype(BF16)
        for k in range(nc):
            proj_ref[k, pl.ds(r0, sub_rows), :] = yb[:, k * CHUNK:(k + 1) * CHUNK]

    @pl.when(j == 0)
    def _():
        hm = _modnorm(x_ref[...], g_ref, mod_ref).astype(BF16)
        hm_s[...] = hm
        _scan_tables(hm, wdt_ref, dtb_ref, alog_ref, tab_ref)

    @pl.when(j < gate_blocks)
    def _():
        for r0 in range(0, rows, sub_rows):
            emit(r0, _silu(_dot_nt(wt_ref[r0:r0 + sub_rows, :], hm_s[...])))

    @pl.when(j >= gate_blocks)
    def _():
        lane0 = lax.broadcasted_iota(jnp.int32, (sub_rows, LANES), 1)

        @pl.loop(0, rows // sub_rows)
        def _(i):
            r0 = pl.multiple_of(i * sub_rows, sub_rows)
            p = _dot_nt(wt_ref[pl.ds(r0, sub_rows), :], hm_s[...])
            left = pltpu.roll(p, 1, 1)
            right = pltpu.roll(p, nt - 1, 1)
            lcols = [left[:, q:q + LANES] for q in range(0, nt, LANES)]
            rcols = [right[:, q:q + LANES] for q in range(0, nt, LANES)]
            for q in range(0, nt, seq_len):
                lcols[q // LANES] = jnp.where(lane0 == 0, 0.0, lcols[q // LANES])
                e = (q + seq_len) // LANES - 1
                rcols[e] = jnp.where(lane0 == LANES - 1, 0.0, rcols[e])
            left = jnp.concatenate(lcols, axis=1)
            right = jnp.concatenate(rcols, axis=1)
            cw = cw_ref[pl.ds(r0, sub_rows), :]
            y = left * cw[:, 0:1] + p * cw[:, 1:2] + right * cw[:, 2:3] + cw[:, 3:4]
            emit(r0, _silu(y))


def _scan_tables(hm, wdt_ref, dtb_ref, alog_ref, tab_ref):
    v = _dot_nt(wdt_ref[...], hm) + dtb_ref[...]
    dt_all = jnp.maximum(v, 0.0) + jnp.log(1.0 + jnp.exp(-jnp.abs(v)))
    heads2, nt = dt_all.shape
    nc = nt // CHUNK
    la_all = dt_all * -jnp.exp(alog_ref[...])
    dt = jnp.concatenate([dt_all[:, k * CHUNK:(k + 1) * CHUNK] for k in range(nc)], axis=0)
    la = jnp.concatenate([la_all[:, k * CHUNK:(k + 1) * CHUNK] for k in range(nc)], axis=0)
    p0 = la.astype(BF16)
    r1 = la - p0.astype(F32)
    p1 = r1.astype(BF16)
    p2 = (r1 - p1.astype(F32)).astype(BF16)
    ji = lax.broadcasted_iota(jnp.int32, (CHUNK, CHUNK), 0)
    li = lax.broadcasted_iota(jnp.int32, (CHUNK, CHUNK), 1)
    upto = jnp.where(ji <= li, 1.0, 0.0).astype(BF16)
    from_ = jnp.where(ji >= li, 1.0, 0.0).astype(BF16)
    pre = _dot(p0, upto) + _dot(p1, upto) + _dot(p2, upto)
    suf = _dot(p0, from_) + _dot(p1, from_) + _dot(p2, from_)
    row = lax.broadcasted_iota(jnp.int32, la.shape, 0)
    is_fwd = (row & (2 * HEADS_PER_GROUP - 1)) < HEADS_PER_GROUP
    cum = jnp.where(is_fwd, pre, suf)
    tot = jnp.where(is_fwd, jnp.broadcast_to(cum[:, CHUNK - 1:CHUNK], cum.shape),
                    jnp.broadcast_to(cum[:, 0:1], cum.shape))
    coef = dt * jnp.exp(tot - cum)
    e = jnp.exp(cum)
    cd = jnp.exp(tot)
    cum2 = cum * LOG2_E
    row2 = cum2 - jnp.maximum(jnp.log2(dt), LOG2_FLOOR)
    for k in range(nc):
        rows = slice(k * heads2, (k + 1) * heads2)
        tab_ref[k, TAB_DT] = dt[rows]
        tab_ref[k, TAB_CUM2] = cum2[rows]
        tab_ref[k, TAB_ROW2] = row2[rows]
        tab_ref[k, TAB_COEF] = coef[rows]
        tab_ref[k, TAB_E] = e[rows]
        tab_ref[k, TAB_CD] = cd[rows]


def _ssd_scan_kernel(sz_ref, x_ref, b_ref, c_ref, tab_ref, dcol_ref, gcol_ref,
                     *rest, seq_chunks, has_h0, has_prev, emit_state, unroll):
    rest = list(rest)
    h0_ref = rest.pop(0) if has_h0 else None
    prev_ref = rest.pop(0) if has_prev else None
    y_ref = rest.pop(0)
    st_ref = rest.pop(0) if emit_state else None
    yp_s, stf_s, stb_s, hf_s, hb_s, seg_s, cb_s, bm_s = rest

    nc, gw, _ = x_ref.shape
    n = b_ref.shape[1]
    hp = gw // HEADS_PER_GROUP
    nh = HEADS_PER_GROUP
    r2 = 2 * nh
    assert not has_h0 or nc == seq_chunks

    li = lax.broadcasted_iota(jnp.int32, (CHUNK, CHUNK), 0)
    si = lax.broadcasted_iota(jnp.int32, (CHUNK, CHUNK), 1)

    def expand(rows4):
        return jnp.concatenate(
            [jnp.broadcast_to(rows4[h:h + 1, :], (hp, LANES)) for h in range(nh)], axis=0)

    blk = (lax.broadcasted_iota(jnp.int32, (r2, r2 * CHUNK), 1) // CHUNK
           == lax.broadcasted_iota(jnp.int32, (r2, r2 * CHUNK), 0))
    sel_k = jnp.concatenate([jnp.where(blk, 1.0, 0.0)] * 3, axis=0)
    ones_k = jnp.ones((3 * r2, CHUNK), F32)

    def split3(v):
        p0 = v.astype(BF16).astype(F32)
        p1 = (v - p0).astype(BF16).astype(F32)
        return p0, p1, v - p0 - p1

    def seg_chunk(c, carry):
        lhs = jnp.concatenate([-p for p in split3(tab_ref[c, TAB_ROW2])] + [ones_k], axis=0)
        rhs = jnp.concatenate(
            [sel_k] + [jnp.where(blk, jnp.concatenate([p] * r2, axis=1), 0.0)
                       for p in split3(tab_ref[c, TAB_CUM2])], axis=0)
        seg_s[c] = lax.dot_general(lhs.astype(BF16), rhs.astype(BF16), (((0,), (0,)), ((), ())),
                                   preferred_element_type=F32)
        bm = b_ref[c].T
        bm_s[c] = bm
        cb_s[c] = _dot(bm, c_ref[c])
        return carry

    lax.fori_loop(0, nc, seg_chunk, 0, unroll=unroll)

    def local_chunk(c, carry):
        xst_b = x_ref[c]
        xst = xst_b.astype(F32)
        cb = cb_s[c]
        parts = []
        for h in range(nh):
            hb = nh + h
            arg = jnp.where(li <= si, seg_s[c, :, h * CHUNK:(h + 1) * CHUNK],
                            seg_s[c, :, hb * CHUNK:(hb + 1) * CHUNK])
            w = (cb * jnp.exp2(arg)).astype(BF16)
            parts.append(_dot(xst_b[h * hp:(h + 1) * hp, :], w))
        cb_diag = jnp.sum(jnp.where(si == li, cb, 0.0), axis=0, keepdims=True)
        skip = dcol_ref[...] + expand(tab_ref[c, TAB_DT][nh:r2] * cb_diag)
        yp_s[c] = jnp.concatenate(parts, axis=0) + skip * xst

        coef8 = tab_ref[c, TAB_COEF]
        xdw = jnp.concatenate([xst * expand(coef8[0:nh]), xst * expand(coef8[nh:r2])], axis=0)
        st = _dot(xdw.astype(BF16), bm_s[c])
        stf_s[c] = st[0:gw]
        stb_s[c] = st[gw:2 * gw]
        return carry

    lax.fori_loop(0, nc, local_chunk, 0, unroll=unroll)

    if has_h0:
        h0f = h0_ref[0].reshape(gw, n)
        h0b = h0_ref[1].reshape(gw, n)
    else:
        h0f = jnp.zeros((gw, n), F32)
        h0b = h0f
    if has_prev:
        st_ref[:, 0:prev_ref.shape[1]] = prev_ref[...]
    slot = st_ref.shape[1] - 1 if emit_state else None

    def fwd_state(c, h):
        hf_s[c] = h.astype(BF16)
        return h * expand(tab_ref[c, TAB_CD][0:nh]) + stf_s[c]

    def bwd_state(c, h):
        hb_s[c] = h.astype(BF16)
        return h * expand(tab_ref[c, TAB_CD][nh:r2]) + stb_s[c]

    for q in range(nc // seq_chunks):
        c0, c1 = q * seq_chunks, (q + 1) * seq_chunks
        hf = lax.fori_loop(c0, c1, fwd_state, h0f)
        hb = lax.fori_loop(0, seq_chunks, lambda i, h: bwd_state(c1 - 1 - i, h), h0b)
        if emit_state:
            st_ref[q, slot, 0] = hf.reshape(nh, hp, n)
            st_ref[q, slot, 1] = hb.reshape(nh, hp, n)

    def output_chunk(c, carry):
        hin = jnp.concatenate([hf_s[c], hb_s[c]], axis=0)
        yo = _dot(hin, c_ref[c])
        e8 = tab_ref[c, TAB_E]
        y_t = yp_s[c] + yo[0:gw] * expand(e8[0:nh]) + yo[gw:2 * gw] * expand(e8[nh:r2])
        y = y_t * sz_ref[c].astype(F32)
        inv = lax.rsqrt(jnp.mean(y * y, axis=0, keepdims=True) + EPS)
        y_ref[c] = (y * inv * gcol_ref[...]).astype(BF16)
        return carry

    lax.fori_loop(0, nc, output_chunk, 0, unroll=unroll)


def _ssd_out_kernel(y_ref, w_ref, x_ref, mod_ref, *rest, final):
    fg_ref = rest[0] if final else None
    o_ref = rest[-1]
    y = jnp.concatenate([y_ref[k].T for k in range(y_ref.shape[0])], axis=0)
    out = _dot(y, w_ref[...])
    o_ref[...] = _residual(x_ref[...], out, mod_ref, fg_ref)


def _ssd_layer(x2d, seq, mod3, mod_row, norm_g, w_zxbc, wdt_t, dt_bias_col, alog_col,
               conv_wb, d_col, gn_g, w_out, h0, prev_state, layer_idx, emit_state, final_g):
    t, d = x2d.shape
    d_inner = w_out.shape[1]
    heads2 = wdt_t.shape[1]
    groups = heads2 // (2 * HEADS_PER_GROUP)
    gw = d_inner // groups
    n = (w_zxbc.shape[1] - 2 * d_inner) // (2 * groups)
    hp = gw // HEADS_PER_GROUP
    ln = seq.length
    nc = ln // CHUNK
    tm = _token_tile(seq, t, SSD_OUT_TILE)
    tiles_per_seq = max(ln // tm, 1)
    row_of_tile = lambda i: mod_row(i // tiles_per_seq)
    const = lambda *_: (0, 0)
    tok = pl.BlockSpec((tm, d), lambda i: (i, 0))
    wrows = w_zxbc.shape[1]

    nt = ln * max(1, min(SSD_IN_TOKENS, t) // ln) if seq.shared_cond else ln
    seqs_per_block = nt // ln
    rows_blk = min(SSD_IN_ROWS, d_inner)
    assert t % nt == 0 and d_inner % rows_blk == 0 and wrows % rows_blk == 0
    n_tab = TAB_CD + 1
    gate_blocks = d_inner // rows_blk
    proj, tab = pl.pallas_call(
        functools.partial(_ssd_in_kernel, seq_len=ln, gate_blocks=gate_blocks,
                          sub_rows=min(SSD_IN_SUB_ROWS, rows_blk)),
        grid=(t // nt, wrows // rows_blk),
        in_specs=[
            pl.BlockSpec((nt, d), lambda i, j: (i, 0)),
            pl.BlockSpec((1, d), const),
            pl.BlockSpec((None, 1, 3 * d), lambda i, j: (mod_row(i * seqs_per_block), 0, 0)),
            pl.BlockSpec((None, rows_blk, d), lambda i, j: (layer_idx, j, 0)),
            pl.BlockSpec((None, heads2, d), lambda i, j: (layer_idx, 0, 0)),
            pl.BlockSpec((heads2, 1), const),
            pl.BlockSpec((heads2, 1), const),
            pl.BlockSpec((rows_blk, 4), lambda i, j: (jnp.maximum(j - gate_blocks, 0), 0)),
        ],
        out_specs=[
            pl.BlockSpec((nt // CHUNK, rows_blk, CHUNK), lambda i, j: (i, j, 0)),
            pl.BlockSpec((nt // CHUNK, n_tab, heads2, CHUNK), lambda i, j: (i, 0, 0, 0)),
        ],
        out_shape=[
            jax.ShapeDtypeStruct((t // CHUNK, wrows, CHUNK), BF16),
            jax.ShapeDtypeStruct((t // CHUNK, n_tab, heads2, CHUNK), F32),
        ],
        scratch_shapes=[pltpu.VMEM((nt, d), BF16)],
        compiler_params=_params(2),
        name="ssd_in",
    )(x2d, norm_g[None], mod3, w_zxbc, wdt_t, dt_bias_col, alog_col, conv_wb)

    r2 = 2 * HEADS_PER_GROUP
    xg0 = d_inner // gw
    bg0 = 2 * d_inner // n
    cg0 = bg0 + groups
    has_h0 = h0 is not None
    has_prev = emit_state and prev_state is not None
    spc = 1 if has_h0 else max(1, min(seq.count, SCAN_CELL_CHUNKS // nc))
    assert seq.count % spc == 0
    cc = spc * nc
    nh = HEADS_PER_GROUP
    in_specs = [
        pl.BlockSpec((cc, gw, CHUNK), lambda s, g: (s, g, 0)),
        pl.BlockSpec((cc, gw, CHUNK), lambda s, g: (s, xg0 + g, 0)),
        pl.BlockSpec((cc, n, CHUNK), lambda s, g: (s, bg0 + g, 0)),
        pl.BlockSpec((cc, n, CHUNK), lambda s, g: (s, cg0 + g, 0)),
        pl.BlockSpec((cc, n_tab, r2, CHUNK), lambda s, g: (s, 0, g, 0)),
        pl.BlockSpec((gw, LANES), lambda s, g: (g, 0)),
        pl.BlockSpec((gw, LANES), lambda s, g: (g, 0)),
    ]
    args = [proj, proj, proj, proj, tab, jnp.broadcast_to(d_col, (d_inner, LANES)),
            jnp.broadcast_to(gn_g[:, None], (d_inner, LANES))]
    if has_h0:
        in_specs.append(pl.BlockSpec((None, None, 2, nh, hp, n),
                                     lambda s, g: (s, layer_idx, 0, g, 0, 0)))
        args.append(h0)
    if has_prev:
        in_specs.append(pl.BlockSpec((spc, prev_state.shape[1], 2, nh, hp, n),
                                     lambda s, g: (s, 0, 0, g, 0, 0)))
        args.append(prev_state)
    out_specs = [pl.BlockSpec((cc, gw, CHUNK), lambda s, g: (s, g, 0))]
    out_shape = [jax.ShapeDtypeStruct((t // CHUNK, d_inner, CHUNK), BF16)]
    if emit_state:
        slots = (prev_state.shape[1] if has_prev else 0) + 1
        out_specs.append(pl.BlockSpec((spc, slots, 2, nh, hp, n), lambda s, g: (s, 0, 0, g, 0, 0)))
        out_shape.append(jax.ShapeDtypeStruct((seq.count, slots, 2, groups * nh, hp, n), F32))
    res = pl.pallas_call(
        functools.partial(_ssd_scan_kernel, seq_chunks=nc, has_h0=has_h0, has_prev=has_prev,
                          emit_state=emit_state, unroll=min(16, cc)),
        grid=(seq.count // spc, groups),
        in_specs=in_specs,
        out_specs=out_specs,
        out_shape=out_shape,
        scratch_shapes=[
            pltpu.VMEM((cc, gw, CHUNK), F32),
            pltpu.VMEM((cc, gw, n), F32),
            pltpu.VMEM((cc, gw, n), F32),
            pltpu.VMEM((cc, gw, n), BF16),
            pltpu.VMEM((cc, gw, n), BF16),
            pltpu.VMEM((cc, CHUNK, r2 * CHUNK), F32),
            pltpu.VMEM((cc, CHUNK, CHUNK), F32),
            pltpu.VMEM((cc, CHUNK, n), BF16),
        ],
        compiler_params=_params(2),
        name="ssd_scan",
    )(*args)
    y, state = (res[0], res[1]) if emit_state else (res[0], None)

    final = final_g is not None
    in_specs = [
        pl.BlockSpec((tm // CHUNK, d_inner, CHUNK), lambda i: (i, 0, 0)),
        pl.BlockSpec((None, d_inner, d), lambda i: (layer_idx, 0, 0), pipeline_mode=pl.Buffered(1)),
        tok,
        pl.BlockSpec((None, 1, 3 * d), lambda i: (row_of_tile(i), 0, 0)),
    ]
    args = [y, w_out, x2d, mod3]
    if final:
        in_specs.append(pl.BlockSpec((1, d), const))
        args.append(final_g[None])
    x_new = pl.pallas_call(
        functools.partial(_ssd_out_kernel, final=final),
        grid=(t // tm,),
        in_specs=in_specs,
        out_specs=tok,
        out_shape=jax.ShapeDtypeStruct((t, d), F32),
        compiler_params=_params(1),
        name="ssd_out",
    )(*args)
    return x_new, state


class _Seqs:
    def __init__(self, count, length, shared_cond):
        self.count = count
        self.length = length
        self.shared_cond = shared_cond


def _trunk(x, mod, mod_row, shared_cond, row_len, h0, emit_state, weights):
    (norm_g, fc_w_in, fc_conv_w, fc_w_out, ssd_w, ssd_wdt_t, ssd_dtb, ssd_alog, ssd_conv_wb,
     ssd_dcol, ssd_norm_g, ssd_w_out, final_norm_g) = weights
    bsz, ln, d = x.shape
    seq = _Seqs(bsz, ln, shared_cond)
    x2d = x.reshape(bsz * ln, d)
    depth = norm_g.shape[0]
    states = None
    for layer in range(depth):
        i = layer // 2
        mod3 = mod[layer].reshape(COND_ROWS, 1, 3 * d)
        final_g = final_norm_g if layer == depth - 1 else None
        if layer % 2 == 0:
            x2d = _fc_layer(x2d, seq, mod3, mod_row, norm_g[layer], fc_w_in, fc_conv_w[i],
                            fc_w_out, i, row_len, final_g)
        else:
            x2d, st = _ssd_layer(x2d, seq, mod3, mod_row, norm_g[layer], ssd_w, ssd_wdt_t,
                                 ssd_dtb[i], ssd_alog[i], ssd_conv_wb[i], ssd_dcol[i],
                                 ssd_norm_g[i], ssd_w_out, h0, states, i, emit_state, final_g)
            states = st
    return x2d.reshape(bsz, ln, d), states


def kernel(x_prompt, x_sample, state_ssm, c, c_ctx, w_mod, b_mod, norm_g, fc_w_in, fc_conv_w,
           fc_w_out, ssd_w_in, ssd_conv_w, ssd_conv_b, ssd_dt_bias, ssd_a_log, ssd_d,
           ssd_norm_g, ssd_w_out, final_norm_g):
    d = x_prompt.shape[-1]
    dec_batch = x_sample.shape[0]
    no, _, heads = ssd_a_log.shape
    d_inner = ssd_w_out.shape[1]
    conv_dim = ssd_conv_w.shape[-1]
    hp = d_inner // heads
    groups = heads // HEADS_PER_GROUP
    assert dec_batch + 1 <= COND_ROWS and heads % HEADS_PER_GROUP == 0

    cond = jnp.zeros((COND_ROWS, d), F32).at[:dec_batch].set(c).at[dec_batch].set(c_ctx)
    mod = _modulation(cond, w_mod, b_mod)

    zx = d_inner + conv_dim
    ssd_w = jnp.swapaxes(ssd_w_in[:, :, :zx], 1, 2).astype(BF16)
    ssd_conv_wb = jnp.swapaxes(jnp.concatenate([ssd_conv_w, ssd_conv_b[:, None]], axis=1), 1, 2)
    perm = jnp.arange(2 * heads).reshape(2, groups, HEADS_PER_GROUP).transpose(1, 0, 2).reshape(-1)
    ssd_wdt_t = jnp.swapaxes(ssd_w_in[:, :, zx:][:, :, perm], 1, 2).astype(BF16)
    ssd_dtb = ssd_dt_bias.reshape(no, 2 * heads)[:, perm][:, :, None]
    ssd_alog = ssd_a_log.reshape(no, 2 * heads)[:, perm][:, :, None]
    ssd_dcol = jnp.repeat(ssd_d, hp, axis=1)[:, :, None]
    weights = (norm_g, fc_w_in.astype(BF16), fc_conv_w, fc_w_out.astype(BF16), ssd_w, ssd_wdt_t,
               ssd_dtb, ssd_alog, ssd_conv_wb, ssd_dcol, ssd_norm_g,
               ssd_w_out.astype(BF16), final_norm_g)

    y_prompt, states = _trunk(x_prompt, mod, lambda s: dec_batch, True, x_prompt.shape[1], None,
                              True, weights)
    y_sample, _ = _trunk(x_sample, mod, lambda s: s, False, GRID_W, state_ssm, False, weights)
    return y_prompt, y_sample, states
```
